```python
import jax, jax.numpy as jnp
from jax import lax
import numpy as np

D_MODEL = 2048
BATCH = 4
SEQ = 4096
DEPTH = 2

SB_HEADS = 16
SB_HEAD_DIM = 64
SB_WIDTH = SB_HEADS * SB_HEAD_DIM
SB_BLOCK = 128
HG_HEADS = 8
HG_KEY_DIM = 128
HG_VAL_DIM = 128
HG_KEY_WIDTH = HG_HEADS * HG_KEY_DIM
HG_VAL_WIDTH = HG_HEADS * HG_VAL_DIM
HG_CHUNK = 64
LB_FLOOR = 1e-20
RW_HEADS = 16
RW_HEAD_DIM = 64
RW_WIDTH = RW_HEADS * RW_HEAD_DIM
RW_DECAY_LORA = 96
RW_AAA_LORA = 96
RW_MV_LORA = 64
RW_GATE_LORA = 256
RW_GN_EPS = 64e-5
L2_EPS = 1e-12
D_FF = 5632
N_BRANCH = 3
NORM_EPS = 1e-6
SB_COLS = 3 * SB_WIDTH
HG_COLS = 2 * HG_KEY_WIDTH + 2 * HG_VAL_WIDTH
RW_COLS = 3 * RW_WIDTH + RW_DECAY_LORA + RW_AAA_LORA + RW_GATE_LORA
GATE_COLS = N_BRANCH * D_MODEL
N_IN = SB_COLS + HG_COLS + RW_COLS + GATE_COLS

kernel_name = "hybrid_sb_hgrn2_rwkv7_macaron"


def split_cols(p, sizes):
    offs = np.cumsum(sizes)[:-1].tolist()
    return jnp.split(p, offs, axis=-1)


def rms_norm(x, gain):
    xf = x.astype(jnp.float32)
    y = xf * lax.rsqrt(jnp.mean(xf * xf, axis=-1, keepdims=True) + NORM_EPS)
    return (y * gain.astype(jnp.float32)).astype(x.dtype)


def swiglu(u, w_gate, w_up, w_down):
    hid = jax.nn.silu(jnp.einsum("btd,df->btf", u, w_gate)) * jnp.einsum("btd,df->btf", u, w_up)
    return jnp.einsum("btf,fd->btd", hid, w_down)


def stick_breaking_attention(q, k, v):
    B, T, H, Dh = q.shape
    scale = Dh ** -0.5
    outs = []
    for start in range(0, T, SB_BLOCK):
        end = start + SB_BLOCK
        z = jnp.einsum("bqhd,bkhd->bhqk", q[:, start:end], k[:, :end],
                       preferred_element_type=jnp.float32) * scale
        q_pos = start + jnp.arange(SB_BLOCK)[:, None]
        k_pos = jnp.arange(end)[None, :]
        causal = k_pos < q_pos
        log_beta = jax.nn.log_sigmoid(z)
        log_1m_beta = jnp.where(causal, jax.nn.log_sigmoid(-z), 0.0)
        suffix = lax.cumsum(log_1m_beta, axis=3, reverse=True) - log_1m_beta
        weights = jnp.where(causal, jnp.exp(jnp.where(causal, log_beta + suffix, 0.0)), 0.0)
        outs.append(jnp.einsum("bhqk,bkhd->bqhd", weights.astype(v.dtype), v[:, :end]))
    return jnp.concatenate(outs, axis=1)


def hgrn2_chunkwise(q, k, log_f, v):
    B, T, H, K = q.shape
    V = v.shape[-1]
    n = T // HG_CHUNK

    def to_chunks(a):
        return a.reshape(B, n, HG_CHUNK, H, a.shape[-1]).transpose(1, 0, 3, 2, 4)

    causal = jnp.tril(jnp.ones((HG_CHUNK, HG_CHUNK), dtype=bool))[:, :, None]

    def step(S, inp):
        qi, ki, gi, vi = inp
        b = jnp.cumsum(gi, axis=2)
        o_inter = jnp.einsum("bhtk,bhkv->bhtv", qi * jnp.exp(b), S)
        diff = b[:, :, :, None, :] - b[:, :, None, :, :]
        decay = jnp.where(causal, jnp.exp(jnp.where(causal, diff, 0.0)), 0.0)
        scores = jnp.einsum("bhtk,bhtsk,bhsk->bhts", qi, decay, ki)
        o_intra = jnp.einsum("bhts,bhsv->bhtv", scores, vi)
        b_last = b[:, :, -1:, :]
        S_new = (jnp.exp(b_last[:, :, 0, :])[..., None] * S
                 + jnp.einsum("bhsk,bhsv->bhkv", ki * jnp.exp(b_last - b), vi))
        return S_new, o_inter + o_intra

    S0 = jnp.zeros((B, H, K, V), jnp.float32)
    _, o = lax.scan(step, S0, (to_chunks(q), to_chunks(k), to_chunks(log_f), to_chunks(v)))
    return o.transpose(1, 0, 3, 2, 4).reshape(B, T, H, V)


def hgrn2_branch(p, lower_bound, out_norm):
    B, T, _ = p.shape
    f32 = jnp.float32
    q, f_raw, i, g = split_cols(p, [HG_KEY_WIDTH, HG_KEY_WIDTH, HG_VAL_WIDTH, HG_VAL_WIDTH])
    q = jax.nn.silu(q.astype(f32).reshape(B, T, HG_HEADS, HG_KEY_DIM))
    f_raw = f_raw.astype(f32).reshape(B, T, HG_HEADS, HG_KEY_DIM)
    i = i.astype(f32).reshape(B, T, HG_HEADS, HG_VAL_DIM)
    lb = lower_bound.astype(f32).reshape(HG_HEADS, HG_KEY_DIM)
    log_lb = jnp.log(jnp.maximum(lb, LB_FLOOR))
    log_f = jnp.logaddexp(log_lb, jnp.log1p(-lb) + jax.nn.log_sigmoid(f_raw))
    k = (1.0 - lb) * jax.nn.sigmoid(-f_raw)
    o = hgrn2_chunkwise(q, k, log_f, i)
    o = o * lax.rsqrt(jnp.mean(o * o, axis=-1, keepdims=True) + NORM_EPS)
    o = o.reshape(B, T, HG_VAL_WIDTH) * out_norm.astype(f32) * jax.nn.silu(g.astype(f32))
    return o


def rwkv7_scan(r, w, k, v, a, b):
    B, T, H, N = r.shape

    def step(S, inp):
        r_t, w_t, k_t, v_t, a_t, b_t = inp
        sa = jnp.einsum("bhvk,bhk->bhv", S, a_t)
        S = S * w_t[:, :, None, :] + sa[..., None] * b_t[:, :, None, :] + v_t[..., None] * k_t[:, :, None, :]
        return S, jnp.einsum("bhvk,bhk->bhv", S, r_t)

    xs = tuple(jnp.moveaxis(t, 1, 0) for t in (r, w, k, v, a, b))
    S0 = jnp.zeros((B, H, N, N), jnp.float32)
    _, y = lax.scan(step, S0, xs)
    return jnp.moveaxis(y, 0, 1)


def rwkv7_branch(p, v_first, mu, w0, w2, a0, a2, g2, k_k, k_a, r_k, ln_w, ln_b, mv):
    B, T, _ = p.shape
    f32 = jnp.float32
    p = p.astype(f32)
    p_prev = jnp.pad(p[:, :-1], ((0, 0), (1, 0), (0, 0)))
    p = p + (p_prev - p) * mu.astype(f32)
    r, k, v, xw, xa, xg = split_cols(p, [RW_WIDTH] * 3 + [RW_DECAY_LORA, RW_AAA_LORA, RW_GATE_LORA])
    log_w = -jax.nn.softplus(-(w0.astype(f32) + jnp.tanh(xw) @ w2.astype(f32))) - 0.5
    decay = jnp.exp(-jnp.exp(log_w))
    if mv is None:
        v_first = v
    else:
        v0, v1, v2 = (t.astype(f32) for t in mv)
        v = v + (v_first - v) * jax.nn.sigmoid(v0 + (v @ v1) @ v2)
    a = jax.nn.sigmoid(a0.astype(f32) + xa @ a2.astype(f32))
    g = jax.nn.sigmoid(xg) @ g2.astype(f32)

    def heads(t):
        return t.reshape(B, T, RW_HEADS, RW_HEAD_DIM)

    kk = heads(k * k_k.astype(f32))
    kk = kk / jnp.maximum(jnp.sqrt(jnp.sum(kk * kk, axis=-1, keepdims=True)), L2_EPS)
    k = k * (1.0 + (a - 1.0) * k_a.astype(f32))
    r_h, k_h, v_h, a_h = heads(r), heads(k), heads(v), heads(a)
    y = rwkv7_scan(r_h, heads(decay), k_h, v_h, -kk, kk * a_h)
    mean = jnp.mean(y, axis=-1, keepdims=True)
    var = jnp.mean(jnp.square(y - mean), axis=-1, keepdims=True)
    y = ((y - mean) * lax.rsqrt(var + RW_GN_EPS)).reshape(B, T, RW_WIDTH) * ln_w.astype(f32) + ln_b.astype(f32)
    y = y + (jnp.sum(r_h * k_h * r_k.astype(f32), axis=-1, keepdims=True) * v_h).reshape(B, T, RW_WIDTH)
    return y * g, v_first


def hybrid_mixer(u, w_in, lower_bound, hg_out_norm, rw_params, rw_mv, v_first,
                 w_up_sb, w_up_hg, w_up_rw, w_o):
    B, T, _ = u.shape
    p = jnp.einsum("btd,de->bte", u, w_in)
    p_sb, p_hg, p_rw, gate_pre = split_cols(p, [SB_COLS, HG_COLS, RW_COLS, GATE_COLS])
    q, k, v = (t.reshape(B, T, SB_HEADS, SB_HEAD_DIM) for t in jnp.split(p_sb, 3, axis=-1))
    y_sb = stick_breaking_attention(q, k, v).reshape(B, T, SB_WIDTH)
    y_hg = hgrn2_branch(p_hg, lower_bound, hg_out_norm).astype(u.dtype)
    y_rw, v_first = rwkv7_branch(p_rw, v_first, *rw_params, rw_mv)
    y_rw = y_rw.astype(u.dtype)
    gates = jax.nn.sigmoid(gate_pre.astype(jnp.float32)).astype(u.dtype).reshape(B, T, N_BRANCH, D_MODEL)
    merged = (gates[:, :, 0] * jnp.einsum("btc,cd->btd", y_sb, w_up_sb)
              + gates[:, :, 1] * jnp.einsum("btc,cd->btd", y_hg, w_up_hg)
              + gates[:, :, 2] * jnp.einsum("btc,cd->btd", y_rw, w_up_rw))
    return jnp.einsum("btd,de->bte", merged, w_o), v_first


def setup_inputs(seed: int = 0) -> dict:
    key = jax.random.key(seed)
    keys = list(jax.random.split(key, 48))
    L, D = DEPTH, D_MODEL

    def nrm(shape, scale):
        return jax.random.normal(keys.pop(), shape, jnp.float32) * scale

    def gain(shape):
        return 1.0 + nrm(shape, 0.02)

    return {
        "x": nrm((BATCH, SEQ, D), 1.0),
        "ffn1_pre_norm": gain((L, D)),
        "ffn1_w_gate": nrm((L, D, D_FF), D ** -0.5),
        "ffn1_w_up": nrm((L, D, D_FF), D ** -0.5),
        "ffn1_w_down": nrm((L, D_FF, D), D_FF ** -0.5),
        "ffn1_post_norm": gain((L, D)),
        "mix_pre_norm": gain((L, D)),
        "w_in": nrm((L, D, N_IN), D ** -0.5),
        "hg_lb_logits": nrm((L, HG_KEY_WIDTH), 0.5),
        "hg_out_norm": gain((L, HG_VAL_WIDTH)),
        "rw_mu": jax.random.uniform(keys.pop(), (L, RW_COLS), jnp.float32),
        "rw_w0": jax.random.uniform(keys.pop(), (L, RW_WIDTH), jnp.float32, minval=-6.0, maxval=1.0),
        "rw_w2": nrm((L, RW_DECAY_LORA, RW_WIDTH), 0.5 * RW_DECAY_LORA ** -0.5),
        "rw_a0": nrm((L, RW_WIDTH), 0.1),
        "rw_a2": nrm((L, RW_AAA_LORA, RW_WIDTH), 0.5 * RW_AAA_LORA ** -0.5),
        "rw_g2": nrm((L, RW_GATE_LORA, RW_WIDTH), RW_GATE_LORA ** -0.5),
        "rw_v0": 1.0 + nrm((L - 1, RW_WIDTH), 0.1),
        "rw_v1": nrm((L - 1, RW_WIDTH, RW_MV_LORA), RW_WIDTH ** -0.5),
        "rw_v2": nrm((L - 1, RW_MV_LORA, RW_WIDTH), 0.5 * RW_MV_LORA ** -0.5),
        "rw_k_k": 0.85 + nrm((L, RW_WIDTH), 0.05),
        "rw_k_a": 1.0 + nrm((L, RW_WIDTH), 0.05),
        "rw_r_k": nrm((L, RW_HEADS, RW_HEAD_DIM), 0.1),
        "rw_ln_w": gain((L, RW_WIDTH)),
        "rw_ln_b": nrm((L, RW_WIDTH), 0.02),
        "w_up_sb": nrm((L, SB_WIDTH, D), SB_WIDTH ** -0.5),
        "w_up_hg": nrm((L, HG_VAL_WIDTH, D), HG_VAL_WIDTH ** -0.5),
        "w_up_rw": nrm((L, RW_WIDTH, D), RW_WIDTH ** -0.5),
        "w_o": nrm((L, D, D), D ** -0.5),
        "mix_post_norm": gain((L, D)),
        "ffn2_pre_norm": gain((L, D)),
        "ffn2_w_gate": nrm((L, D, D_FF), D ** -0.5),
        "ffn2_w_up": nrm((L, D, D_FF), D ** -0.5),
        "ffn2_w_down": nrm((L, D_FF, D), D_FF ** -0.5),
        "ffn2_post_norm": gain((L, D)),
    }


def reference(x, ffn1_pre_norm, ffn1_w_gate, ffn1_w_up, ffn1_w_down, ffn1_post_norm,
              mix_pre_norm, w_in, hg_lb_logits, hg_out_norm,
              rw_mu, rw_w0, rw_w2, rw_a0, rw_a2, rw_g2, rw_v0, rw_v1, rw_v2,
              rw_k_k, rw_k_a, rw_r_k, rw_ln_w, rw_ln_b,
              w_up_sb, w_up_hg, w_up_rw, w_o, mix_post_norm,
              ffn2_pre_norm, ffn2_w_gate, ffn2_w_up, ffn2_w_down, ffn2_post_norm):
    lb_w = jax.nn.softmax(hg_lb_logits.astype(jnp.float32), axis=0)
    lower_bounds = jnp.cumsum(lb_w, axis=0) - lb_w[0]
    h = x
    v_first = None
    for l in range(DEPTH):
        f1 = swiglu(rms_norm(h, ffn1_pre_norm[l]), ffn1_w_gate[l], ffn1_w_up[l], ffn1_w_down[l])
        h = h + 0.5 * rms_norm(f1, ffn1_post_norm[l])
        rw_params = (rw_mu[l], rw_w0[l], rw_w2[l], rw_a0[l], rw_a2[l], rw_g2[l],
                     rw_k_k[l], rw_k_a[l], rw_r_k[l], rw_ln_w[l], rw_ln_b[l])
        rw_mv = None if l == 0 else (rw_v0[l - 1], rw_v1[l - 1], rw_v2[l - 1])
        m, v_first = hybrid_mixer(rms_norm(h, mix_pre_norm[l]), w_in[l], lower_bounds[l], hg_out_norm[l],
                                  rw_params, rw_mv, v_first, w_up_sb[l], w_up_hg[l], w_up_rw[l], w_o[l])
        h = h + rms_norm(m, mix_post_norm[l])
        f2 = swiglu(rms_norm(h, ffn2_pre_norm[l]), ffn2_w_gate[l], ffn2_w_up[l], ffn2_w_down[l])
        h = h + 0.5 * rms_norm(f2, ffn2_post_norm[l])
    return h
```

```python
import functools

import jax
import jax.numpy as jnp
from jax import lax
from jax.experimental import pallas as pl
from jax.experimental.pallas import tpu as pltpu

F32 = jnp.float32
BF16 = jnp.bfloat16

D_MODEL = 2048
D_FF = 5632
NORM_EPS = 1e-6
SB_HEADS = 16
SB_HEAD_DIM = 64
SB_WIDTH = SB_HEADS * SB_HEAD_DIM
HG_HEADS = 8
HG_DIM = 128
HG_WIDTH = HG_HEADS * HG_DIM
LB_FLOOR = 1e-20
RW_HEADS = 16
RW_HEAD_DIM = 64
RW_WIDTH = RW_HEADS * RW_HEAD_DIM
RW_DECAY_LORA = 96
RW_AAA_LORA = 96
RW_GATE_LORA = 256
RW_LORA = RW_DECAY_LORA + RW_AAA_LORA + RW_GATE_LORA
RW_LORA_PAD = 512
RW_MV_LORA = 64
RW_MV_PAD = 128
RW_GN_EPS = 64e-5
L2_EPS = 1e-12
N_BRANCH = 3

LANES = 128
VMEM_LIMIT = 56 * 1024 * 1024

FFN_TM = 512
FFN_TF = 512
MM_TM = 1024
MM_TN = 512
OUT_TM = 256
SB_T = 128
HG_TC = 256
HG_SUB = 16
RW_TC = 128
RW_LW = 256
RW_GROUP = 8
PREP_TM = 256


def _cparams(*sem):
    return pltpu.CompilerParams(dimension_semantics=sem, vmem_limit_bytes=VMEM_LIMIT)


def _split2(x):
    hi = x.astype(BF16)
    lo = (x - hi.astype(F32)).astype(BF16)
    return hi, lo


def _split3(x):
    hi = x.astype(BF16)
    r = x - hi.astype(F32)
    mid = r.astype(BF16)
    lo = (r - mid.astype(F32)).astype(BF16)
    return hi, mid, lo


def _dot(a, b):
    return jnp.dot(a, b, preferred_element_type=F32)


def _rms(x, gain):
    return x * lax.rsqrt(jnp.mean(x * x, axis=-1, keepdims=True) + NORM_EPS) * gain


def _softplus(z):
    return jnp.maximum(z, 0.0) + jnp.log(1.0 + jnp.exp(-jnp.abs(z)))


def _sigmoid(z):
    return 1.0 / (1.0 + jnp.exp(-z))


def _pair_ones():
    r = lax.broadcasted_iota(jnp.int32, (LANES, LANES), 0) // RW_HEAD_DIM
    c = lax.broadcasted_iota(jnp.int32, (LANES, LANES), 1) // RW_HEAD_DIM
    return (r == c).astype(BF16)


def _seg64_sum(x, pair):
    hi, lo = _split2(x)
    return _dot(hi, pair) + _dot(lo, pair)


def _ffn_kernel(h_ref, gpre_ref, wg_ref, wu_ref, wd_ref, gpost_ref, o_ref, u_ref):
    f = pl.program_id(1)

    @pl.when(f == 0)
    def _():
        u_ref[...] = _rms(h_ref[...], gpre_ref[...]).astype(BF16)
        o_ref[...] = jnp.zeros_like(o_ref)

    u = u_ref[...]
    g = _dot(u, wg_ref[...])
    up = _dot(u, wu_ref[...])
    hid = (g * _sigmoid(g) * up).astype(BF16)
    o_ref[...] += _dot(hid, wd_ref[...])

    @pl.when(f == pl.num_programs(1) - 1)
    def _():
        o_ref[...] = h_ref[...] + 0.5 * _rms(o_ref[...], gpost_ref[...])


def _ffn(h, gpre, wg, wu, wd, gpost):
    m, d = h.shape
    ff = wg.shape[1]
    tm = min(FFN_TM, m)
    return pl.pallas_call(
        _ffn_kernel,
        grid=(m // tm, ff // FFN_TF),
        in_specs=[
            pl.BlockSpec((tm, d), lambda i, f: (i, 0)),
            pl.BlockSpec((1, d), lambda i, f: (0, 0)),
            pl.BlockSpec((d, FFN_TF), lambda i, f: (0, f)),
            pl.BlockSpec((d, FFN_TF), lambda i, f: (0, f)),
            pl.BlockSpec((FFN_TF, d), lambda i, f: (f, 0)),
            pl.BlockSpec((1, d), lambda i, f: (0, 0)),
        ],
        out_specs=pl.BlockSpec((tm, d), lambda i, f: (i, 0)),
        out_shape=jax.ShapeDtypeStruct((m, d), F32),
        scratch_shapes=[pltpu.VMEM((tm, d), BF16)],
        compiler_params=_cparams("parallel", "arbitrary"),
        name="ffn",
    )(h, gpre, wg, wu, wd, gpost)


def _norm_kernel(x_ref, g_ref, o_ref):
    o_ref[...] = _rms(x_ref[...], g_ref[...]).astype(o_ref.dtype)


def _norm(x, gain):
    m, d = x.shape
    tm = min(MM_TM, m)
    return pl.pallas_call(
        _norm_kernel,
        grid=(m // tm,),
        in_specs=[pl.BlockSpec((tm, d), lambda i: (i, 0)), pl.BlockSpec((1, d), lambda i: (0, 0))],
        out_specs=pl.BlockSpec((tm, d), lambda i: (i, 0)),
        out_shape=jax.ShapeDtypeStruct((m, d), BF16),
        compiler_params=_cparams("parallel"),
        name="mix_norm",
    )(x, gain)


def _mm_kernel(x_ref, w_ref, o_ref, *, act):
    y = _dot(x_ref[...], w_ref[...])
    if act == "sigmoid":
        y = _sigmoid(y)
    o_ref[...] = y.astype(o_ref.dtype)


def _mm(x, w, out_dtype, act=None, name="mm"):
    m, k = x.shape
    n = w.shape[1]
    tm = min(MM_TM, m)
    tn = min(MM_TN, n)
    return pl.pallas_call(
        functools.partial(_mm_kernel, act=act),
        grid=(m // tm, n // tn),
        in_specs=[pl.BlockSpec((tm, k), lambda i, j: (i, 0)), pl.BlockSpec((k, tn), lambda i, j: (0, j))],
        out_specs=pl.BlockSpec((tm, tn), lambda i, j: (i, j)),
        out_shape=jax.ShapeDtypeStruct((m, n), out_dtype),
        compiler_params=_cparams("parallel", "parallel"),
        name=name,
    )(x, w)


def _sb_kernel(q_ref, k_ref, v_ref, o_ref, acc_ref, car_ref):
    i = pl.program_id(2)
    t = SB_T
    lane = lax.broadcasted_iota(jnp.int32, (t, LANES), 1)
    head0 = lane < SB_HEAD_DIM
    q = q_ref[0] * jnp.asarray(SB_HEAD_DIM ** -0.5, BF16)
    zero = jnp.zeros_like(q)
    qh = (jnp.where(head0, q, zero), jnp.where(head0, zero, q))
    row = lax.broadcasted_iota(jnp.int32, (t, t), 0)
    col = lax.broadcasted_iota(jnp.int32, (t, t), 1)
    causal = col < row
    r2 = lax.broadcasted_iota(jnp.int32, (t, 2 * t), 0)
    c2 = lax.broadcasted_iota(jnp.int32, (t, 2 * t), 1)
    tri = ((r2 > c2) | (c2 >= t)).astype(BF16)

    acc_ref[...] = jnp.zeros_like(acc_ref)
    car_ref[...] = jnp.zeros_like(car_ref)

    def block(kb, masked):
        start = pl.multiple_of(kb * t, t)
        kblk = k_ref[0, pl.ds(start, t), :]
        vblk = v_ref[0, pl.ds(start, t), :]
        for h in range(2):
            z = lax.dot_general(qh[h], kblk, (((1,), (1,)), ((), ())), preferred_element_type=F32)
            sp = _softplus(z)
            l1m = -sp
            if masked:
                l1m = jnp.where(causal, l1m, 0.0)
            hi, lo = _split2(l1m)
            s2 = _dot(hi, tri) + _dot(lo, tri)
            logw = (z - sp) + s2[:, :t] + car_ref[h]
            w = jnp.exp(logw)
            if masked:
                w = jnp.where(causal, w, 0.0)
            acc_ref[h] += _dot(w.astype(BF16), vblk)
            car_ref[h] += s2[:, t:]

    block(i, True)

    def body(j, c):
        block(i - 1 - j, False)
        return c

    lax.fori_loop(0, i, body, 0)
    o_ref[0] = jnp.where(head0, acc_ref[0], acc_ref[1]).astype(o_ref.dtype)


def _sb_attention(p_sb, b, t):
    npair = SB_WIDTH // LANES
    return pl.pallas_call(
        _sb_kernel,
        grid=(b, npair, t // SB_T),
        in_specs=[
            pl.BlockSpec((1, SB_T, LANES), lambda bi, hp, i: (bi, i, hp)),
            pl.BlockSpec((1, t, LANES), lambda bi, hp, i: (bi, 0, npair + hp)),
            pl.BlockSpec((1, t, LANES), lambda bi, hp, i: (bi, 0, 2 * npair + hp)),
        ],
        out_specs=pl.BlockSpec((1, SB_T, LANES), lambda bi, hp, i: (bi, i, hp)),
        out_shape=jax.ShapeDtypeStruct((b, t, SB_WIDTH), BF16),
        scratch_shapes=[pltpu.VMEM((2, SB_T, LANES), F32), pltpu.VMEM((2, SB_T, SB_T), F32)],
        compiler_params=_cparams("parallel", "parallel", "arbitrary"),
        name="sb_attention",
    )(p_sb, p_sb, p_sb)


def _hg_kernel(q_ref, f_ref, i_ref, g_ref, lb_ref, on_ref, o_ref, st_ref):
    c = pl.program_id(2)
    n = HG_SUB

    @pl.when(c == 0)
    def _():
        st_ref[...] = jnp.zeros_like(st_ref)

    lb = lb_ref[0]
    log_lb = jnp.log(jnp.maximum(lb, LB_FLOOR))
    log_1m_lb = jnp.log(1.0 - lb)
    one_m_lb = 1.0 - lb
    r = lax.broadcasted_iota(jnp.int32, (n, n), 0)
    cidx = lax.broadcasted_iota(jnp.int32, (n, n), 1)
    ltri = (cidx <= r).astype(BF16)
    rown = lax.broadcasted_iota(jnp.int32, (n, 1), 0)

    def sub(j, carry):
        s0 = pl.multiple_of(j * n, n)
        qr = q_ref[0, pl.ds(s0, n), :]
        fr = f_ref[0, pl.ds(s0, n), :]
        vi = i_ref[0, pl.ds(s0, n), :]
        q = qr * _sigmoid(qr)
        sp = _softplus(-fr)
        x = log_1m_lb - sp
        mx = jnp.maximum(log_lb, x)
        log_f = mx + jnp.log(1.0 + jnp.exp(-jnp.abs(log_lb - x)))
        k = one_m_lb * _sigmoid(-fr)
        hi, mid, lo = _split3(log_f)
        bc = _dot(ltri, hi) + _dot(ltri, mid) + _dot(ltri, lo)
        b_end = bc[n - 1:n, :]
        st = st_ref[...]
        qd = q * jnp.exp(bc)
        o = lax.dot_general(qd.astype(BF16), st.astype(BF16), (((1,), (1,)), ((), ())),
                            preferred_element_type=F32)
        for s in range(n):
            d = jnp.exp(jnp.minimum(bc - bc[s:s + 1, :], 0.0))
            col = jnp.sum(q * d * k[s:s + 1, :], axis=-1, keepdims=True)
            col = jnp.where(rown >= s, col, 0.0)
            o = o + col * vi[s:s + 1, :]
        ke = k * jnp.exp(b_end - bc)
        upd = lax.dot_general(vi.astype(BF16), ke.astype(BF16), (((0,), (0,)), ((), ())),
                              preferred_element_type=F32)
        st_ref[...] = st * jnp.exp(b_end) + upd
        o = o * lax.rsqrt(jnp.mean(o * o, axis=-1, keepdims=True) + NORM_EPS)
        gr = g_ref[0, pl.ds(s0, n), :]
        o_ref[0, pl.ds(s0, n), :] = (o * on_ref[0] * (gr * _sigmoid(gr))).astype(o_ref.dtype)
        return carry

    lax.fori_loop(0, HG_TC // n, sub, 0)


def _hgrn2(p_hg, lb, out_norm, b, t):
    tc = min(HG_TC, t)
    nh = HG_HEADS

    def spec(off):
        return pl.BlockSpec((1, tc, LANES), lambda bi, h, c: (bi, c, off * nh + h))

    pspec = pl.BlockSpec((1, 1, LANES), lambda bi, h, c: (h, 0, 0))
    return pl.pallas_call(
        _hg_kernel,
        grid=(b, nh, t // tc),
        in_specs=[spec(0), spec(1), spec(2), spec(3), pspec, pspec],
        out_specs=pl.BlockSpec((1, tc, LANES), lambda bi, h, c: (bi, c, h)),
        out_shape=jax.ShapeDtypeStruct((b, t, HG_WIDTH), BF16),
        scratch_shapes=[pltpu.VMEM((HG_DIM, HG_DIM), F32)],
        compiler_params=_cparams("parallel", "parallel", "arbitrary"),
        name="hgrn2",
    )(p_hg, p_hg, p_hg, p_hg, lb.reshape(nh, 1, LANES), out_norm.reshape(nh, 1, LANES))


def _rw_prep_kernel(*refs, t_len, mix):
    if mix:
        (x_ref, xp_ref, l_ref, lp_ref, mux_ref, mul_ref, w0_ref, a0_ref, kk_ref, ka_ref,
         w2_ref, a2_ref, g2_ref, vf_ref, v0_ref, v1_ref, v2_ref,
         r_o, w_o, k_o, v_o, an_o, bn_o, g_o) = refs
    else:
        (x_ref, xp_ref, l_ref, lp_ref, mux_ref, mul_ref, w0_ref, a0_ref, kk_ref, ka_ref,
         w2_ref, a2_ref, g2_ref,
         r_o, w_o, k_o, v_o, an_o, bn_o, g_o) = refs
    i = pl.program_id(0)
    tm = x_ref.shape[0]
    keep = ((i * tm) % t_len != 0).astype(F32)
    row0 = lax.broadcasted_iota(jnp.int32, (tm, 1), 0) == 0

    def shift(cur, prev8, mu):
        prev = jnp.where(row0, prev8[7:8, :] * keep, pltpu.roll(cur, 1, 0))
        return cur + (prev - cur) * mu

    x = shift(x_ref[...], xp_ref[...], mux_ref[...])
    xl = shift(l_ref[...], lp_ref[...], mul_ref[...])
    r = x[:, :RW_WIDTH]
    k = x[:, RW_WIDTH:2 * RW_WIDTH]
    v = x[:, 2 * RW_WIDTH:]
    lw = w0_ref[...] + _dot(jnp.tanh(xl).astype(BF16), w2_ref[...])
    log_w = -_softplus(-lw) - 0.5
    w_o[...] = jnp.exp(-jnp.exp(log_w))
    a = _sigmoid(a0_ref[...] + _dot(xl.astype(BF16), a2_ref[...]))
    g_o[...] = _dot(_sigmoid(xl).astype(BF16), g2_ref[...])
    if mix:
        vv = _dot(_dot(v.astype(BF16), v1_ref[...]).astype(BF16), v2_ref[...])
        v = v + (vf_ref[...] - v) * _sigmoid(v0_ref[...] + vv)
    pair = _pair_ones()
    kk = k * kk_ref[...]
    sq = kk * kk
    ss = jnp.concatenate([_seg64_sum(sq[:, j * LANES:(j + 1) * LANES], pair)
                          for j in range(RW_WIDTH // LANES)], axis=1)
    kk = kk / jnp.maximum(jnp.sqrt(ss), L2_EPS)
    r_o[...] = r
    k_o[...] = k * (1.0 + (a - 1.0) * ka_ref[...])
    v_o[...] = v
    an_o[...] = -kk
    bn_o[...] = kk * a


def _rw_prep(p_x, p_l, t_len, mu_x, mu_l, w0, a0, k_k, k_a, w2p, a2p, g2p, mv=None):
    m = p_x.shape[0]
    tm = min(PREP_TM, m)
    nb8 = tm // 8
    row = lambda i: (i, 0)
    prev = lambda i: (jnp.maximum(i * nb8 - 1, 0), 0)
    const = lambda i: (0, 0)
    wx, wl = p_x.shape[1], p_l.shape[1]
    vec = pl.BlockSpec((1, RW_WIDTH), const)
    in_specs = [
        pl.BlockSpec((tm, wx), row), pl.BlockSpec((8, wx), prev),
        pl.BlockSpec((tm, wl), row), pl.BlockSpec((8, wl), prev),
        pl.BlockSpec((1, wx), const), pl.BlockSpec((1, wl), const),
        vec, vec, vec, vec,
        pl.BlockSpec((wl, RW_WIDTH), const), pl.BlockSpec((wl, RW_WIDTH), const),
        pl.BlockSpec((wl, RW_WIDTH), const),
    ]
    args = [p_x, p_x, p_l, p_l, mu_x, mu_l, w0, a0, k_k, k_a, w2p, a2p, g2p]
    if mv is not None:
        v_first, v0, v1p, v2p = mv
        in_specs += [pl.BlockSpec((tm, RW_WIDTH), row), vec,
                     pl.BlockSpec((RW_WIDTH, RW_MV_PAD), const), pl.BlockSpec((RW_MV_PAD, RW_WIDTH), const)]
        args += [v_first, v0, v1p, v2p]
    out = jax.ShapeDtypeStruct((m, RW_WIDTH), F32)
    return pl.pallas_call(
        functools.partial(_rw_prep_kernel, t_len=t_len, mix=mv is not None),
        grid=(m // tm,),
        in_specs=in_specs,
        out_specs=[pl.BlockSpec((tm, RW_WIDTH), row)] * 7,
        out_shape=[out] * 7,
        compiler_params=_cparams("parallel"),
        name="rwkv7_prep",
    )(*args)


def _rw_scan_kernel(r_ref, w_ref, k_ref, v_ref, a_ref, b_ref, g_ref, rk_ref, lnw_ref, lnb_ref,
                    o_ref, st_ref, y_ref):
    c = pl.program_id(1)
    nb = r_ref.shape[0]
    tc = r_ref.shape[1]
    nl = r_ref.shape[2] // LANES
    n = RW_HEAD_DIM

    @pl.when(c == 0)
    def _():
        st_ref[...] = jnp.zeros_like(st_ref)

    pair = _pair_ones()
    ri = lax.broadcasted_iota(jnp.int32, (n, LANES), 0)
    ci = lax.broadcasted_iota(jnp.int32, (n, LANES), 1)
    eye = (ri == ci % n).astype(F32)

    def bsum(x):
        hi, lo = _split2(x)
        return _dot(hi, pair) + _dot(lo, pair)

    row8 = lax.broadcasted_iota(jnp.int32, (RW_GROUP, LANES), 0)

    def group(tg, carry):
        t0 = pl.multiple_of(tg * RW_GROUP, RW_GROUP)
        for bi in range(nb):
            for j in range(nl):
                ls = pl.ds(j * LANES, LANES)
                r8, w8, k8, v8, a8, b8 = (ref[bi, pl.ds(t0, RW_GROUP), ls]
                                          for ref in (r_ref, w_ref, k_ref, v_ref, a_ref, b_ref))
                ch = bi * nl + j
                s = st_ref[ch]
                y8 = jnp.zeros((RW_GROUP, LANES), F32)
                for i in range(RW_GROUP):
                    row = lambda x8: x8[i:i + 1, :]
                    sa = bsum(s * row(a8))
                    vcol = bsum(eye * row(v8))
                    s = s * row(w8) + sa * row(b8) + vcol * row(k8)
                    yb = bsum(s * row(r8))
                    y8 = jnp.where(row8 == i, jnp.sum(yb * eye, axis=0, keepdims=True), y8)
                st_ref[ch] = s
                y_ref[ch, pl.ds(t0, RW_GROUP), :] = y8
        return carry

    lax.fori_loop(0, tc // RW_GROUP, group, 0)

    inv_n = 1.0 / n
    for bi in range(nb):
        for j in range(nl):
            ls = pl.ds(j * LANES, LANES)
            y = y_ref[bi * nl + j]
            mean = bsum(y) * inv_n
            d = y - mean
            var = bsum(d * d) * inv_n
            y = d * lax.rsqrt(var + RW_GN_EPS) * lnw_ref[0, :, ls] + lnb_ref[0, :, ls]
            bonus = bsum(r_ref[bi, :, ls] * k_ref[bi, :, ls] * rk_ref[0, :, ls])
            y = y + bonus * v_ref[bi, :, ls]
            o_ref[bi, :, ls] = (y * g_ref[bi, :, ls]).astype(o_ref.dtype)


def _rw_scan(r, w, k, v, an, bn, g, r_k, ln_w, ln_b):
    b, t, width = r.shape
    tc = min(RW_TC, t)
    ngrp = width // RW_LW
    nchain = b * (RW_LW // LANES)
    seq = pl.BlockSpec((b, tc, RW_LW), lambda hg, c: (0, c, hg))
    par = pl.BlockSpec((1, 1, RW_LW), lambda hg, c: (hg, 0, 0))
    resh = lambda p: p.reshape(ngrp, 1, RW_LW)
    return pl.pallas_call(
        _rw_scan_kernel,
        grid=(ngrp, t // tc),
        in_specs=[seq] * 7 + [par] * 3,
        out_specs=seq,
        out_shape=jax.ShapeDtypeStruct((b, t, width), BF16),
        scratch_shapes=[pltpu.VMEM((nchain, RW_HEAD_DIM, LANES), F32), pltpu.VMEM((nchain, tc, LANES), F32)],
        compiler_params=_cparams("parallel", "arbitrary"),
        name="rwkv7_scan",
    )(r, w, k, v, an, bn, g, resh(r_k), resh(ln_w), resh(ln_b))


def _merge_kernel(ysb_ref, yhg_ref, yrw_ref, wsb_ref, whg_ref, wrw_ref, g0_ref, g1_ref, g2_ref, o_ref):
    m = g0_ref[...] * _dot(ysb_ref[...], wsb_ref[...])
    m += g1_ref[...] * _dot(yhg_ref[...], whg_ref[...])
    m += g2_ref[...] * _dot(yrw_ref[...], wrw_ref[...])
    o_ref[...] = m.astype(o_ref.dtype)


def _merge(y_sb, y_hg, y_rw, w_sb, w_hg, w_rw, gates):
    m, kdim = y_sb.shape
    d = w_sb.shape[1]
    tm = min(MM_TM, m)
    tn = MM_TN
    nj = d // tn
    ys = pl.BlockSpec((tm, kdim), lambda i, j: (i, 0))
    ws = pl.BlockSpec((kdim, tn), lambda i, j: (0, j))

    def gs(br):
        return pl.BlockSpec((tm, tn), lambda i, j: (i, br * nj + j))

    return pl.pallas_call(
        _merge_kernel,
        grid=(m // tm, nj),
        in_specs=[ys, ys, ys, ws, ws, ws, gs(0), gs(1), gs(2)],
        out_specs=pl.BlockSpec((tm, tn), lambda i, j: (i, j)),
        out_shape=jax.ShapeDtypeStruct((m, d), BF16),
        compiler_params=_cparams("parallel", "parallel"),
        name="branch_merge",
    )(y_sb, y_hg, y_rw, w_sb, w_hg, w_rw, gates, gates, gates)


def _outproj_kernel(x_ref, w_ref, gain_ref, h_ref, o_ref):
    o_ref[...] = h_ref[...] + _rms(_dot(x_ref[...], w_ref[...]), gain_ref[...])


def _outproj(x, w, gain, h):
    m, d = h.shape
    tm = min(OUT_TM, m)
    row = lambda i: (i, 0)
    const = lambda i: (0, 0)
    return pl.pallas_call(
        _outproj_kernel,
        grid=(m // tm,),
        in_specs=[pl.BlockSpec((tm, d), row), pl.BlockSpec((d, d), const), pl.BlockSpec((1, d), const),
                  pl.BlockSpec((tm, d), row)],
        out_specs=pl.BlockSpec((tm, d), row),
        out_shape=jax.ShapeDtypeStruct((m, d), F32),
        compiler_params=_cparams("parallel"),
        name="out_proj",
    )(x, w, gain, h)


def _pad_rows(w, start, total):
    return jnp.zeros((total, w.shape[1]), w.dtype).at[start:start + w.shape[0]].set(w)


def _mixer(h, b, t, l, gain, w_in, lower_bound, hg_out_norm, rw, mv, v_first, w_up_sb, w_up_hg, w_up_rw, w_o,
           post_gain):
    (mu, w0, w2, a0, a2, g2, k_k, k_a, r_k, ln_w, ln_b) = rw
    m = b * t
    row = lambda p: p.reshape(1, -1)
    u = _norm(h, row(gain))
    o_sb = 3 * SB_WIDTH
    o_hg = o_sb + 4 * HG_WIDTH
    o_rx = o_hg + 3 * RW_WIDTH
    o_rl = o_rx + RW_LORA
    wb = w_in.astype(BF16)
    p_sb = _mm(u, wb[:, :o_sb], BF16, name="in_proj_sb")
    p_hg = _mm(u, wb[:, o_sb:o_hg], F32, name="in_proj_hg")
    p_rx = _mm(u, wb[:, o_hg:o_rx], F32, name="in_proj_rw")
    w_l = jnp.pad(wb[:, o_rx:o_rl], ((0, 0), (0, RW_LORA_PAD - RW_LORA)))
    p_rl = _mm(u, w_l, F32, name="in_proj_rw_lora")
    gates = _mm(u, wb[:, o_rl:], F32, act="sigmoid", name="in_proj_gates")

    y_sb = _sb_attention(p_sb.reshape(b, t, o_sb), b, t).reshape(m, SB_WIDTH)
    y_hg = _hgrn2(p_hg.reshape(b, t, 4 * HG_WIDTH), lower_bound, hg_out_norm, b, t).reshape(m, HG_WIDTH)

    mu_x = row(mu[:3 * RW_WIDTH])
    mu_l = row(jnp.pad(mu[3 * RW_WIDTH:], (0, RW_LORA_PAD - RW_LORA)))
    w2p = _pad_rows(w2, 0, RW_LORA_PAD).astype(BF16)
    a2p = _pad_rows(a2, RW_DECAY_LORA, RW_LORA_PAD).astype(BF16)
    g2p = _pad_rows(g2, RW_DECAY_LORA + RW_AAA_LORA, RW_LORA_PAD).astype(BF16)
    if mv is None:
        mvp = None
    else:
        v0, v1, v2 = mv
        mvp = (v_first, row(v0), jnp.pad(v1, ((0, 0), (0, RW_MV_PAD - RW_MV_LORA))).astype(BF16),
               _pad_rows(v2, 0, RW_MV_PAD).astype(BF16))
    r, w, k, v, an, bn, g = _rw_prep(p_rx, p_rl, t, mu_x, mu_l, row(w0), row(a0), row(k_k), row(k_a),
                                     w2p, a2p, g2p, mvp)
    if mv is None:
        v_first = v
    sh = lambda z: z.reshape(b, t, RW_WIDTH)
    y_rw = _rw_scan(sh(r), sh(w), sh(k), sh(v), sh(an), sh(bn), sh(g), r_k.reshape(-1), ln_w, ln_b)
    y_rw = y_rw.reshape(m, RW_WIDTH)

    merged = _merge(y_sb, y_hg, y_rw, w_up_sb.astype(BF16), w_up_hg.astype(BF16), w_up_rw.astype(BF16), gates)
    return _outproj(merged, w_o.astype(BF16), row(post_gain), h), v_first


def kernel(x, ffn1_pre_norm, ffn1_w_gate, ffn1_w_up, ffn1_w_down, ffn1_post_norm, mix_pre_norm, w_in, hg_lb_logits, hg_out_norm, rw_mu, rw_w0, rw_w2, rw_a0, rw_a2, rw_g2, rw_v0, rw_v1, rw_v2, rw_k_k, rw_k_a, rw_r_k, rw_ln_w, rw_ln_b, w_up_sb, w_up_hg, w_up_rw, w_o, mix_post_norm, ffn2_pre_norm, ffn2_w_gate, ffn2_w_up, ffn2_w_down, ffn2_post_norm):
    b, t, d = x.shape
    depth = w_in.shape[0]
    lb_w = jax.nn.softmax(hg_lb_logits.astype(F32), axis=0)
    lower_bounds = jnp.cumsum(lb_w, axis=0) - lb_w[0]
    row = lambda p: p.reshape(1, -1)
    h = x.reshape(b * t, d)
    v_first = None
    for l in range(depth):
        h = _ffn(h, row(ffn1_pre_norm[l]), ffn1_w_gate[l].astype(BF16), ffn1_w_up[l].astype(BF16),
                 ffn1_w_down[l].astype(BF16), row(ffn1_post_norm[l]))
        rw = (rw_mu[l], rw_w0[l], rw_w2[l], rw_a0[l], rw_a2[l], rw_g2[l], rw_k_k[l], rw_k_a[l], rw_r_k[l],
              rw_ln_w[l], rw_ln_b[l])
        mv = None if l == 0 else (rw_v0[l - 1], rw_v1[l - 1], rw_v2[l - 1])
        h, v_first = _mixer(h, b, t, l, mix_pre_norm[l], w_in[l], lower_bounds[l], hg_out_norm[l], rw, mv, v_first,
                            w_up_sb[l], w_up_hg[l], w_up_rw[l], w_o[l], mix_post_norm[l])
        h = _ffn(h, row(ffn2_pre_norm[l]), ffn2_w_gate[l].astype(BF16), ffn2_w_up[l].astype(BF16),
                 ffn2_w_down[l].astype(BF16), row(ffn2_post_norm[l]))
    return h.reshape(b, t, d)
```

```python
import functools

import jax
import jax.numpy as jnp
from jax import lax
from jax.experimental import pallas as pl
from jax.experimental.pallas import tpu as pltpu

F32 = jnp.float32
BF16 = jnp.bfloat16

D_MODEL = 2048
D_FF = 5632
NORM_EPS = 1e-6
SB_HEADS = 16
SB_HEAD_DIM = 64
SB_WIDTH = SB_HEADS * SB_HEAD_DIM
HG_HEADS = 8
HG_DIM = 128
HG_WIDTH = HG_HEADS * HG_DIM
LB_FLOOR = 1e-20
RW_HEADS = 16
RW_HEAD_DIM = 64
RW_WIDTH = RW_HEADS * RW_HEAD_DIM
RW_DECAY_LORA = 96
RW_AAA_LORA = 96
RW_GATE_LORA = 256
RW_LORA = RW_DECAY_LORA + RW_AAA_LORA + RW_GATE_LORA
RW_LORA_PAD = 512
RW_MV_LORA = 64
RW_MV_PAD = 128
RW_GN_EPS = 64e-5
L2_EPS = 1e-12
N_BRANCH = 3

LANES = 128
VMEM_LIMIT = 56 * 1024 * 1024

FFN_TM = 512
FFN_TF = 512
MM_TM = 1024
MM_TN = 512
OUT_TM = 256
SB_TQ = 512
SB_TK = 128
HG_TC = 256
HG_SUB = 16
RW_TC = 256
RW_CHUNK = 64
RW_INV_STEPS = 5
PREP_TM = 256


def _cparams(*sem):
    return pltpu.CompilerParams(dimension_semantics=sem, vmem_limit_bytes=VMEM_LIMIT)


def _split2(x):
    hi = x.astype(BF16)
    lo = (x - hi.astype(F32)).astype(BF16)
    return hi, lo


def _split3(x):
    hi = x.astype(BF16)
    r = x - hi.astype(F32)
    mid = r.astype(BF16)
    lo = (r - mid.astype(F32)).astype(BF16)
    return hi, mid, lo


def _dot(a, b):
    return jnp.dot(a, b, preferred_element_type=F32)


def _nt(a, b):
    return lax.dot_general(a, b, (((1,), (1,)), ((), ())), preferred_element_type=F32)


def _tn(a, b):
    return lax.dot_general(a, b, (((0,), (0,)), ((), ())), preferred_element_type=F32)


def _rms(x, gain):
    return x * lax.rsqrt(jnp.mean(x * x, axis=-1, keepdims=True) + NORM_EPS) * gain


def _softplus(z):
    return jnp.maximum(z, 0.0) + jnp.log(1.0 + jnp.exp(-jnp.abs(z)))


def _sigmoid(z):
    return 1.0 / (1.0 + jnp.exp(-z))


def _pair_ones():
    r = lax.broadcasted_iota(jnp.int32, (LANES, LANES), 0) // RW_HEAD_DIM
    c = lax.broadcasted_iota(jnp.int32, (LANES, LANES), 1) // RW_HEAD_DIM
    return (r == c).astype(BF16)


def _seg64_sum(x, pair):
    hi, lo = _split2(x)
    return _dot(hi, pair) + _dot(lo, pair)


def _ffn_kernel(h_ref, gpre_ref, wg_ref, wu_ref, wd_ref, gpost_ref, o_ref, u_ref):
    f = pl.program_id(1)

    @pl.when(f == 0)
    def _():
        u_ref[...] = _rms(h_ref[...], gpre_ref[...]).astype(BF16)
        o_ref[...] = jnp.zeros_like(o_ref)

    u = u_ref[...]
    g = _dot(u, wg_ref[...])
    up = _dot(u, wu_ref[...])
    hid = (g * _sigmoid(g) * up).astype(BF16)
    o_ref[...] += _dot(hid, wd_ref[...])

    @pl.when(f == pl.num_programs(1) - 1)
    def _():
        o_ref[...] = h_ref[...] + 0.5 * _rms(o_ref[...], gpost_ref[...])


def _ffn(h, gpre, wg, wu, wd, gpost):
    m, d = h.shape
    ff = wg.shape[1]
    tm = min(FFN_TM, m)
    return pl.pallas_call(
        _ffn_kernel,
        grid=(m // tm, ff // FFN_TF),
        in_specs=[
            pl.BlockSpec((tm, d), lambda i, f: (i, 0)),
            pl.BlockSpec((1, d), lambda i, f: (0, 0)),
            pl.BlockSpec((d, FFN_TF), lambda i, f: (0, f)),
            pl.BlockSpec((d, FFN_TF), lambda i, f: (0, f)),
            pl.BlockSpec((FFN_TF, d), lambda i, f: (f, 0)),
            pl.BlockSpec((1, d), lambda i, f: (0, 0)),
        ],
        out_specs=pl.BlockSpec((tm, d), lambda i, f: (i, 0)),
        out_shape=jax.ShapeDtypeStruct((m, d), F32),
        scratch_shapes=[pltpu.VMEM((tm, d), BF16)],
        compiler_params=_cparams("parallel", "arbitrary"),
        name="ffn",
    )(h, gpre, wg, wu, wd, gpost)


def _norm_kernel(x_ref, g_ref, o_ref):
    o_ref[...] = _rms(x_ref[...], g_ref[...]).astype(o_ref.dtype)


def _norm(x, gain):
    m, d = x.shape
    tm = min(MM_TM, m)
    return pl.pallas_call(
        _norm_kernel,
        grid=(m // tm,),
        in_specs=[pl.BlockSpec((tm, d), lambda i: (i, 0)), pl.BlockSpec((1, d), lambda i: (0, 0))],
        out_specs=pl.BlockSpec((tm, d), lambda i: (i, 0)),
        out_shape=jax.ShapeDtypeStruct((m, d), BF16),
        compiler_params=_cparams("parallel"),
        name="mix_norm",
    )(x, gain)


def _mm_kernel(x_ref, w_ref, o_ref, *, act):
    y = _dot(x_ref[...], w_ref[...])
    if act == "sigmoid":
        y = _sigmoid(y)
    o_ref[...] = y.astype(o_ref.dtype)


def _mm(x, w, out_dtype, act=None, name="mm"):
    m, k = x.shape
    n = w.shape[1]
    tm = min(MM_TM, m)
    tn = min(MM_TN, n)
    return pl.pallas_call(
        functools.partial(_mm_kernel, act=act),
        grid=(m // tm, n // tn),
        in_specs=[pl.BlockSpec((tm, k), lambda i, j: (i, 0)), pl.BlockSpec((k, tn), lambda i, j: (0, j))],
        out_specs=pl.BlockSpec((tm, tn), lambda i, j: (i, j)),
        out_shape=jax.ShapeDtypeStruct((m, n), out_dtype),
        compiler_params=_cparams("parallel", "parallel"),
        name=name,
    )(x, w)


def _sb_kernel(q_ref, k_ref, v_ref, o_ref, acc_ref, car_ref):
    i = pl.program_id(2)
    tq, tk = SB_TQ, SB_TK
    nkb = tq // tk
    head0 = lax.broadcasted_iota(jnp.int32, (tq, LANES), 1) < SB_HEAD_DIM
    q = q_ref[0] * jnp.asarray(SB_HEAD_DIM ** -0.5, BF16)
    zero = jnp.zeros_like(q)
    qh = (jnp.where(head0, q, zero), jnp.where(head0, zero, q))
    row = lax.broadcasted_iota(jnp.int32, (tq, tk), 0)
    col = lax.broadcasted_iota(jnp.int32, (tq, tk), 1)
    tri = (lax.broadcasted_iota(jnp.int32, (tk, tk), 0) > lax.broadcasted_iota(jnp.int32, (tk, tk), 1)).astype(BF16)

    acc_ref[...] = jnp.zeros_like(acc_ref)
    car_ref[...] = jnp.zeros_like(car_ref)

    def group(g, masked):
        start = pl.multiple_of(g * tq, tq)
        kg = k_ref[0, pl.ds(start, tq), :]
        vg = v_ref[0, pl.ds(start, tq), :]
        order = [(h, u) for u in reversed(range(nkb)) for h in range(2)]
        zg = [_nt(qh[h], kg) for h in range(2)]
        z = {(h, u): zg[h][:, u * tk:(u + 1) * tk] for h, u in order}
        nsp = {c: jnp.minimum(-z[c], 0.0) - jnp.log(1.0 + jnp.exp(-jnp.abs(z[c]))) for c in order}
        if masked:
            causal = {u: col + u * tk < row for u in range(nkb)}
            l1m = {c: jnp.where(causal[c[1]], nsp[c], 0.0) for c in order}
        else:
            l1m = nsp
        parts = {c: _split2(l1m[c]) for c in order}
        suf = {c: _dot(parts[c][0], tri) + _dot(parts[c][1], tri) for c in order}
        tot = {c: jnp.sum(l1m[c], axis=1, keepdims=True) for c in order}
        for h in range(2):
            car = car_ref[h]
            w = [None] * nkb
            for u in reversed(range(nkb)):
                wu = jnp.exp((z[h, u] + nsp[h, u]) + suf[h, u] + car)
                if masked:
                    wu = jnp.where(causal[u], wu, 0.0)
                w[u] = wu.astype(BF16)
                car = car + tot[h, u]
            car_ref[h] = car
            acc_ref[h] += _dot(jnp.concatenate(w, axis=1), vg)

    group(i, True)

    def body(j, c):
        group(i - 1 - j, False)
        return c

    lax.fori_loop(0, i, body, 0)
    o_ref[0] = jnp.where(head0, acc_ref[0], acc_ref[1]).astype(o_ref.dtype)


def _sb_attention(p_sb, b, t):
    npair = SB_WIDTH // LANES
    return pl.pallas_call(
        _sb_kernel,
        grid=(b, npair, t // SB_TQ),
        in_specs=[
            pl.BlockSpec((1, SB_TQ, LANES), lambda bi, hp, i: (bi, i, hp)),
            pl.BlockSpec((1, t, LANES), lambda bi, hp, i: (bi, 0, npair + hp)),
            pl.BlockSpec((1, t, LANES), lambda bi, hp, i: (bi, 0, 2 * npair + hp)),
        ],
        out_specs=pl.BlockSpec((1, SB_TQ, LANES), lambda bi, hp, i: (bi, i, hp)),
        out_shape=jax.ShapeDtypeStruct((b, t, SB_WIDTH), BF16),
        scratch_shapes=[pltpu.VMEM((2, SB_TQ, LANES), F32), pltpu.VMEM((2, SB_TQ, 1), F32)],
        compiler_params=_cparams("parallel", "parallel", "arbitrary"),
        name="sb_attention",
    )(p_sb, p_sb, p_sb)


def _hg_kernel(q_ref, f_ref, i_ref, g_ref, lb_ref, on_ref, o_ref, st_ref):
    c = pl.program_id(1)
    n = HG_SUB
    chains = range(q_ref.shape[0])

    @pl.when(c == 0)
    def _():
        st_ref[...] = jnp.zeros_like(st_ref)

    lb = lb_ref[0]
    log_lb = jnp.log(jnp.maximum(lb, LB_FLOOR))
    log_1m_lb = jnp.log(1.0 - lb)
    one_m_lb = 1.0 - lb
    r = lax.broadcasted_iota(jnp.int32, (n, n), 0)
    cidx = lax.broadcasted_iota(jnp.int32, (n, n), 1)
    ltri = (cidx <= r).astype(BF16)
    rown = lax.broadcasted_iota(jnp.int32, (n, 1), 0)

    def sub(j, carry):
        s0 = pl.multiple_of(j * n, n)
        sl = [(bi, pl.ds(s0, n), slice(None)) for bi in chains]
        qr = [q_ref[s] for s in sl]
        fr = [f_ref[s] for s in sl]
        vi = [i_ref[s] for s in sl]
        q = [x * _sigmoid(x) for x in qr]
        xs = [log_1m_lb - _softplus(-x) for x in fr]
        log_f = [jnp.maximum(log_lb, x) + jnp.log(1.0 + jnp.exp(-jnp.abs(log_lb - x))) for x in xs]
        k = [one_m_lb * _sigmoid(-x) for x in fr]
        parts = [_split3(x) for x in log_f]
        bc = [_dot(ltri, p[0]) + _dot(ltri, p[1]) + _dot(ltri, p[2]) for p in parts]
        b_end = [x[n - 1:n, :] for x in bc]
        st = [st_ref[bi] for bi in chains]
        o = [_nt((q[bi] * jnp.exp(bc[bi])).astype(BF16), st[bi].astype(BF16)) for bi in chains]
        ke = [(k[bi] * jnp.exp(b_end[bi] - bc[bi])).astype(BF16) for bi in chains]
        for bi in chains:
            st_ref[bi] = st[bi] * jnp.exp(b_end[bi]) + _tn(vi[bi].astype(BF16), ke[bi])
        for s in range(n):
            for bi in chains:
                d = jnp.exp(jnp.minimum(bc[bi] - bc[bi][s:s + 1, :], 0.0))
                col = jnp.sum(q[bi] * d * k[bi][s:s + 1, :], axis=-1, keepdims=True)
                o[bi] = o[bi] + jnp.where(rown >= s, col, 0.0) * vi[bi][s:s + 1, :]
        for bi in chains:
            on = o[bi] * lax.rsqrt(jnp.mean(o[bi] * o[bi], axis=-1, keepdims=True) + NORM_EPS)
            gr = g_ref[sl[bi]]
            o_ref[sl[bi]] = (on * on_ref[0] * (gr * _sigmoid(gr))).astype(o_ref.dtype)
        return carry

    lax.fori_loop(0, q_ref.shape[1] // n, sub, 0)


def _hgrn2(p_hg, lb, out_norm, b, t):
    tc = min(HG_TC, t)
    nh = HG_HEADS

    def spec(off):
        return pl.BlockSpec((b, tc, LANES), lambda h, c: (0, c, off * nh + h))

    pspec = pl.BlockSpec((1, 1, LANES), lambda h, c: (h, 0, 0))
    return pl.pallas_call(
        _hg_kernel,
        grid=(nh, t // tc),
        in_specs=[spec(0), spec(1), spec(2), spec(3), pspec, pspec],
        out_specs=pl.BlockSpec((b, tc, LANES), lambda h, c: (0, c, h)),
        out_shape=jax.ShapeDtypeStruct((b, t, HG_WIDTH), BF16),
        scratch_shapes=[pltpu.VMEM((b, HG_DIM, HG_DIM), F32)],
        compiler_params=_cparams("parallel", "arbitrary"),
        name="hgrn2",
    )(p_hg, p_hg, p_hg, p_hg, lb.reshape(nh, 1, LANES), out_norm.reshape(nh, 1, LANES))


def _rw_prep_kernel(*refs, t_len, mix):
    if mix:
        (x_ref, xp_ref, l_ref, lp_ref, mux_ref, mul_ref, w0_ref, a0_ref, kk_ref, ka_ref,
         w2_ref, a2_ref, g2_ref, vf_ref, v0_ref, v1_ref, v2_ref,
         r_o, w_o, k_o, v_o, an_o, bn_o, g_o) = refs
    else:
        (x_ref, xp_ref, l_ref, lp_ref, mux_ref, mul_ref, w0_ref, a0_ref, kk_ref, ka_ref,
         w2_ref, a2_ref, g2_ref,
         r_o, w_o, k_o, v_o, an_o, bn_o, g_o) = refs
    i = pl.program_id(0)
    tm = x_ref.shape[0]
    keep = ((i * tm) % t_len != 0).astype(F32)
    row0 = lax.broadcasted_iota(jnp.int32, (tm, 1), 0) == 0

    def shift(cur, prev8, mu):
        prev = jnp.where(row0, prev8[7:8, :] * keep, pltpu.roll(cur, 1, 0))
        return cur + (prev - cur) * mu

    x = shift(x_ref[...], xp_ref[...], mux_ref[...])
    xl = shift(l_ref[...], lp_ref[...], mul_ref[...])
    r = x[:, :RW_WIDTH]
    k = x[:, RW_WIDTH:2 * RW_WIDTH]
    v = x[:, 2 * RW_WIDTH:]
    lw = w0_ref[...] + _dot(jnp.tanh(xl).astype(BF16), w2_ref[...])
    log_w = -_softplus(-lw) - 0.5
    w_o[...] = -jnp.exp(log_w)
    a = _sigmoid(a0_ref[...] + _dot(xl.astype(BF16), a2_ref[...]))
    g_o[...] = _dot(_sigmoid(xl).astype(BF16), g2_ref[...])
    if mix:
        vv = _dot(_dot(v.astype(BF16), v1_ref[...]).astype(BF16), v2_ref[...])
        v = v + (vf_ref[...] - v) * _sigmoid(v0_ref[...] + vv)
    pair = _pair_ones()
    kk = k * kk_ref[...]
    sq = kk * kk
    ss = jnp.concatenate([_seg64_sum(sq[:, j * LANES:(j + 1) * LANES], pair)
                          for j in range(RW_WIDTH // LANES)], axis=1)
    kk = kk / jnp.maximum(jnp.sqrt(ss), L2_EPS)
    r_o[...] = r
    k_o[...] = k * (1.0 + (a - 1.0) * ka_ref[...])
    v_o[...] = v
    an_o[...] = -kk
    bn_o[...] = kk * a


def _rw_prep(p_x, p_l, t_len, mu_x, mu_l, w0, a0, k_k, k_a, w2p, a2p, g2p, mv=None):
    m = p_x.shape[0]
    tm = min(PREP_TM, m)
    nb8 = tm // 8
    row = lambda i: (i, 0)
    prev = lambda i: (jnp.maximum(i * nb8 - 1, 0), 0)
    const = lambda i: (0, 0)
    wx, wl = p_x.shape[1], p_l.shape[1]
    vec = pl.BlockSpec((1, RW_WIDTH), const)
    in_specs = [
        pl.BlockSpec((tm, wx), row), pl.BlockSpec((8, wx), prev),
        pl.BlockSpec((tm, wl), row), pl.BlockSpec((8, wl), prev),
        pl.BlockSpec((1, wx), const), pl.BlockSpec((1, wl), const),
        vec, vec, vec, vec,
        pl.BlockSpec((wl, RW_WIDTH), const), pl.BlockSpec((wl, RW_WIDTH), const),
        pl.BlockSpec((wl, RW_WIDTH), const),
    ]
    args = [p_x, p_x, p_l, p_l, mu_x, mu_l, w0, a0, k_k, k_a, w2p, a2p, g2p]
    if mv is not None:
        v_first, v0, v1p, v2p = mv
        in_specs += [pl.BlockSpec((tm, RW_WIDTH), row), vec,
                     pl.BlockSpec((RW_WIDTH, RW_MV_PAD), const), pl.BlockSpec((RW_MV_PAD, RW_WIDTH), const)]
        args += [v_first, v0, v1p, v2p]
    out = jax.ShapeDtypeStruct((m, RW_WIDTH), F32)
    return pl.pallas_call(
        functools.partial(_rw_prep_kernel, t_len=t_len, mix=mv is not None),
        grid=(m // tm,),
        in_specs=in_specs,
        out_specs=[pl.BlockSpec((tm, RW_WIDTH), row)] * 7,
        out_shape=[out] * 7,
        compiler_params=_cparams("parallel"),
        name="rwkv7_prep",
    )(*args)


def _rw_scan_kernel(r_ref, lw_ref, k_ref, v_ref, a_ref, b_ref, g_ref, rk_ref, lnw_ref, lnb_ref,
                    o_ref, st_ref):
    c = pl.program_id(1)
    nb = r_ref.shape[0]
    tc = r_ref.shape[1]
    cs = RW_CHUNK
    n = RW_HEAD_DIM

    @pl.when(c == 0)
    def _():
        st_ref[...] = jnp.zeros_like(st_ref)

    pair = _pair_ones()
    head0 = lax.broadcasted_iota(jnp.int32, (cs, LANES), 1) < n
    ri = lax.broadcasted_iota(jnp.int32, (cs, cs), 0)
    ci = lax.broadcasted_iota(jnp.int32, (cs, cs), 1)
    strict = ci < ri
    incl = ci <= ri
    ltri = incl.astype(BF16)
    eye = (ri == ci).astype(F32)
    same_head = (lax.broadcasted_iota(jnp.int32, (LANES, LANES), 0) // n
                 == lax.broadcasted_iota(jnp.int32, (LANES, LANES), 1) // n)
    inv_n = 1.0 / n

    def bsum(x):
        hi, lo = _split2(x)
        return _dot(hi, pair) + _dot(lo, pair)

    def chunk(ic, carry):
        t0 = pl.multiple_of(ic * cs, cs)
        chains = range(nb)
        heads = [(bi, h) for bi in chains for h in range(2)]
        sl = [(bi, pl.ds(t0, cs), slice(None)) for bi in chains]
        r, lw, k, v, a, b = ([ref[s] for s in sl] for ref in (r_ref, lw_ref, k_ref, v_ref, a_ref, b_ref))
        parts = [_split3(x) for x in lw]
        cl = [_dot(ltri, p[0]) + _dot(ltri, p[1]) + _dot(ltri, p[2]) for p in parts]
        cl_end = [x[cs - 1:cs, :] for x in cl]
        e_neg = [jnp.exp(-x) for x in cl]
        e_tail = [jnp.exp(cl_end[i] - cl[i]) for i in chains]
        abar = [(a[i] * jnp.exp(cl[i] - lw[i])).astype(BF16) for i in chains]
        rbar = [(r[i] * jnp.exp(cl[i])).astype(BF16) for i in chains]
        bbar = [(b[i] * e_neg[i]).astype(BF16) for i in chains]
        kbar = [(k[i] * e_neg[i]).astype(BF16) for i in chains]
        bhat = [(b[i] * e_tail[i]).astype(BF16) for i in chains]
        khat = [(k[i] * e_tail[i]).astype(BF16) for i in chains]
        vb = [x.astype(BF16) for x in v]
        zero = jnp.zeros((cs, LANES), BF16)
        not0 = jnp.logical_not(head0)
        ah = {(i, h): jnp.where(head0 if h == 0 else not0, abar[i], zero) for i, h in heads}
        rh = {(i, h): jnp.where(head0 if h == 0 else not0, rbar[i], zero) for i, h in heads}
        mab = {c: jnp.where(strict, _nt(ah[c], bbar[c[0]]), 0.0) for c in heads}
        mak = {c: jnp.where(strict, _nt(ah[c], kbar[c[0]]), 0.0).astype(BF16) for c in heads}
        nrb = {c: jnp.where(incl, _nt(rh[c], bbar[c[0]]), 0.0).astype(BF16) for c in heads}
        nrk = {c: jnp.where(incl, _nt(rh[c], kbar[c[0]]), 0.0).astype(BF16) for c in heads}
        x = {c: eye + mab[c] for c in heads}
        p = mab
        for _ in range(RW_INV_STEPS):
            pb = {c: p[c].astype(BF16) for c in heads}
            p = {c: _dot(pb[c], pb[c]) for c in heads}
            x = {c: x[c] + _dot(x[c].astype(BF16), p[c].astype(BF16)) for c in heads}
        inv = {c: x[c].astype(BF16) for c in heads}
        st = [st_ref[i] for i in chains]
        sb = [s.astype(BF16) for s in st]
        wb = [(_nt(abar[i], sb[i]) + jnp.where(head0, _dot(mak[i, 0], vb[i]), _dot(mak[i, 1], vb[i]))).astype(BF16)
              for i in chains]
        ub = [jnp.where(head0, _dot(inv[i, 0], wb[i]), _dot(inv[i, 1], wb[i])).astype(BF16) for i in chains]
        y = [_nt(rbar[i], sb[i]) + jnp.where(head0, _dot(nrb[i, 0], ub[i]) + _dot(nrk[i, 0], vb[i]),
                                             _dot(nrb[i, 1], ub[i]) + _dot(nrk[i, 1], vb[i])) for i in chains]
        for i in chains:
            st_ref[i] = st[i] * jnp.exp(cl_end[i]) + jnp.where(same_head, _tn(ub[i], bhat[i]) + _tn(vb[i], khat[i]), 0.0)
        mean = [bsum(x) * inv_n for x in y]
        d = [y[i] - mean[i] for i in chains]
        var = [bsum(x * x) * inv_n for x in d]
        bonus = [bsum(r[i] * k[i] * rk_ref[0]) for i in chains]
        for i in chains:
            yn = d[i] * lax.rsqrt(var[i] + RW_GN_EPS) * lnw_ref[0] + lnb_ref[0] + bonus[i] * v[i]
            o_ref[sl[i]] = (yn * g_ref[sl[i]]).astype(o_ref.dtype)
        return carry

    lax.fori_loop(0, tc // cs, chunk, 0)


def _rw_scan(r, lw, k, v, an, bn, g, r_k, ln_w, ln_b):
    b, t, width = r.shape
    tc = min(RW_TC, t)
    ngrp = width // LANES
    seq = pl.BlockSpec((b, tc, LANES), lambda hg, c: (0, c, hg))
    par = pl.BlockSpec((1, 1, LANES), lambda hg, c: (hg, 0, 0))
    resh = lambda p: p.reshape(ngrp, 1, LANES)
    return pl.pallas_call(
        _rw_scan_kernel,
        grid=(ngrp, t // tc),
        in_specs=[seq] * 7 + [par] * 3,
        out_specs=seq,
        out_shape=jax.ShapeDtypeStruct((b, t, width), BF16),
        scratch_shapes=[pltpu.VMEM((b, LANES, LANES), F32)],
        compiler_params=_cparams("parallel", "arbitrary"),
        name="rwkv7_scan",
    )(r, lw, k, v, an, bn, g, resh(r_k), resh(ln_w), resh(ln_b))


def _merge_kernel(ysb_ref, yhg_ref, yrw_ref, wsb_ref, whg_ref, wrw_ref, g0_ref, g1_ref, g2_ref, o_ref):
    m = g0_ref[...] * _dot(ysb_ref[...], wsb_ref[...])
    m += g1_ref[...] * _dot(yhg_ref[...], whg_ref[...])
    m += g2_ref[...] * _dot(yrw_ref[...], wrw_ref[...])
    o_ref[...] = m.astype(o_ref.dtype)


def _merge(y_sb, y_hg, y_rw, w_sb, w_hg, w_rw, gates):
    m, kdim = y_sb.shape
    d = w_sb.shape[1]
    tm = min(MM_TM, m)
    tn = MM_TN
    nj = d // tn
    ys = pl.BlockSpec((tm, kdim), lambda i, j: (i, 0))
    ws = pl.BlockSpec((kdim, tn), lambda i, j: (0, j))

    def gs(br):
        return pl.BlockSpec((tm, tn), lambda i, j: (i, br * nj + j))

    return pl.pallas_call(
        _merge_kernel,
        grid=(m // tm, nj),
        in_specs=[ys, ys, ys, ws, ws, ws, gs(0), gs(1), gs(2)],
        out_specs=pl.BlockSpec((tm, tn), lambda i, j: (i, j)),
        out_shape=jax.ShapeDtypeStruct((m, d), BF16),
        compiler_params=_cparams("parallel", "parallel"),
        name="branch_merge",
    )(y_sb, y_hg, y_rw, w_sb, w_hg, w_rw, gates, gates, gates)


def _outproj_kernel(x_ref, w_ref, gain_ref, h_ref, o_ref):
    o_ref[...] = h_ref[...] + _rms(_dot(x_ref[...], w_ref[...]), gain_ref[...])


def _outproj(x, w, gain, h):
    m, d = h.shape
    tm = min(OUT_TM, m)
    row = lambda i: (i, 0)
    const = lambda i: (0, 0)
    return pl.pallas_call(
        _outproj_kernel,
        grid=(m // tm,),
        in_specs=[pl.BlockSpec((tm, d), row), pl.BlockSpec((d, d), const), pl.BlockSpec((1, d), const),
                  pl.BlockSpec((tm, d), row)],
        out_specs=pl.BlockSpec((tm, d), row),
        out_shape=jax.ShapeDtypeStruct((m, d), F32),
        compiler_params=_cparams("parallel"),
        name="out_proj",
    )(x, w, gain, h)


def _pad_rows(w, start, total):
    return jnp.zeros((total, w.shape[1]), w.dtype).at[start:start + w.shape[0]].set(w)


def _mixer(h, b, t, l, gain, w_in, lower_bound, hg_out_norm, rw, mv, v_first, w_up_sb, w_up_hg, w_up_rw, w_o,
           post_gain):
    (mu, w0, w2, a0, a2, g2, k_k, k_a, r_k, ln_w, ln_b) = rw
    m = b * t
    row = lambda p: p.reshape(1, -1)
    u = _norm(h, row(gain))
    o_sb = 3 * SB_WIDTH
    o_hg = o_sb + 4 * HG_WIDTH
    o_rx = o_hg + 3 * RW_WIDTH
    o_rl = o_rx + RW_LORA
    wb = w_in.astype(BF16)
    p_sb = _mm(u, wb[:, :o_sb], BF16, name="in_proj_sb")
    p_hg = _mm(u, wb[:, o_sb:o_hg], F32, name="in_proj_hg")
    p_rx = _mm(u, wb[:, o_hg:o_rx], F32, name="in_proj_rw")
    w_l = jnp.pad(wb[:, o_rx:o_rl], ((0, 0), (0, RW_LORA_PAD - RW_LORA)))
    p_rl = _mm(u, w_l, F32, name="in_proj_rw_lora")
    gates = _mm(u, wb[:, o_rl:], F32, act="sigmoid", name="in_proj_gates")

    y_sb = _sb_attention(p_sb.reshape(b, t, o_sb), b, t).reshape(m, SB_WIDTH)
    y_hg = _hgrn2(p_hg.reshape(b, t, 4 * HG_WIDTH), lower_bound, hg_out_norm, b, t).reshape(m, HG_WIDTH)

    mu_x = row(mu[:3 * RW_WIDTH])
    mu_l = row(jnp.pad(mu[3 * RW_WIDTH:], (0, RW_LORA_PAD - RW_LORA)))
    w2p = _pad_rows(w2, 0, RW_LORA_PAD).astype(BF16)
    a2p = _pad_rows(a2, RW_DECAY_LORA, RW_LORA_PAD).astype(BF16)
    g2p = _pad_rows(g2, RW_DECAY_LORA + RW_AAA_LORA, RW_LORA_PAD).astype(BF16)
    if mv is None:
        mvp = None
    else:
        v0, v1, v2 = mv
        mvp = (v_first, row(v0), jnp.pad(v1, ((0, 0), (0, RW_MV_PAD - RW_MV_LORA))).astype(BF16),
               _pad_rows(v2, 0, RW_MV_PAD).astype(BF16))
    r, w, k, v, an, bn, g = _rw_prep(p_rx, p_rl, t, mu_x, mu_l, row(w0), row(a0), row(k_k), row(k_a),
                                     w2p, a2p, g2p, mvp)
    if mv is None:
        v_first = v
    sh = lambda z: z.reshape(b, t, RW_WIDTH)
    y_rw = _rw_scan(sh(r), sh(w), sh(k), sh(v), sh(an), sh(bn), sh(g), r_k.reshape(-1), ln_w, ln_b)
    y_rw = y_rw.reshape(m, RW_WIDTH)

    merged = _merge(y_sb, y_hg, y_rw, w_up_sb.astype(BF16), w_up_hg.astype(BF16), w_up_rw.astype(BF16), gates)
    return _outproj(merged, w_o.astype(BF16), row(post_gain), h), v_first


def kernel(x, ffn1_pre_norm, ffn1_w_gate, ffn1_w_up, ffn1_w_down, ffn1_post_norm, mix_pre_norm, w_in, hg_lb_logits, hg_out_norm, rw_mu, rw_w0, rw_w2, rw_a0, rw_a2, rw_g2, rw_v0, rw_v1, rw_v2, rw_k_k, rw_k_a, rw_r_k, rw_ln_w, rw_ln_b, w_up_sb, w_up_hg, w_up_rw, w_o, mix_post_norm, ffn2_pre_norm, ffn2_w_gate, ffn2_w_up, ffn2_w_down, ffn2_post_norm):
    b, t, d = x.shape
    depth = w_in.shape[0]
    lb_w = jax.nn.softmax(hg_lb_logits.astype(F32), axis=0)
    lower_bounds = jnp.cumsum(lb_w, axis=0) - lb_w[0]
    row = lambda p: p.reshape(1, -1)
    h = x.reshape(b * t, d)
    v_first = None
    for l in range(depth):
        h = _ffn(h, row(ffn1_pre_norm[l]), ffn1_w_gate[l].astype(BF16), ffn1_w_up[l].astype(BF16),
                 ffn1_w_down[l].astype(BF16), row(ffn1_post_norm[l]))
        rw = (rw_mu[l], rw_w0[l], rw_w2[l], rw_a0[l], rw_a2[l], rw_g2[l], rw_k_k[l], rw_k_a[l], rw_r_k[l],
              rw_ln_w[l], rw_ln_b[l])
        mv = None if l == 0 else (rw_v0[l - 1], rw_v1[l - 1], rw_v2[l - 1])
        h, v_first = _mixer(h, b, t, l, mix_pre_norm[l], w_in[l], lower_bounds[l], hg_out_norm[l], rw, mv, v_first,
                            w_up_sb[l], w_up_hg[l], w_up_rw[l], w_o[l], mix_post_norm[l])
        h = _ffn(h, row(ffn2_pre_norm[l]), ffn2_w_gate[l].astype(BF16), ffn2_w_up[l].astype(BF16),
                 ffn2_w_down[l].astype(BF16), row(ffn2_post_norm[l]))
    return h.reshape(b, t, d)
```

```python
import functools

import jax
import jax.numpy as jnp
from jax import lax
from jax.experimental import pallas as pl
from jax.experimental.pallas import tpu as pltpu

F32 = jnp.float32
BF16 = jnp.bfloat16

D_MODEL = 2048
D_FF = 5632
NORM_EPS = 1e-6
SB_HEADS = 16
SB_HEAD_DIM = 64
SB_WIDTH = SB_HEADS * SB_HEAD_DIM
HG_HEADS = 8
HG_DIM = 128
HG_WIDTH = HG_HEADS * HG_DIM
LB_FLOOR = 1e-20
RW_HEADS = 16
RW_HEAD_DIM = 64
RW_WIDTH = RW_HEADS * RW_HEAD_DIM
RW_DECAY_LORA = 96
RW_AAA_LORA = 96
RW_GATE_LORA = 256
RW_LORA = RW_DECAY_LORA + RW_AAA_LORA + RW_GATE_LORA
RW_LORA_PAD = 512
RW_MV_LORA = 64
RW_MV_PAD = 128
RW_GN_EPS = 64e-5
L2_EPS = 1e-12
N_BRANCH = 3

LANES = 128
VMEM_LIMIT = 56 * 1024 * 1024

FFN_TM = 512
FFN_TF = 512
MM_TM = 1024
MM_TN = 512
OUT_TM = 256
SB_TQ = 512
SB_TK = 128
HG_TC = 256
HG_SUB = 16
HG_LW = 256
RW_TC = 256
RW_LW = 256
RW_CHUNK = 64
RW_INV_STEPS = 5
PREP_TM = 256
CAST_BLOCK_BYTES = 6 * 1024 * 1024


def _cparams(*sem):
    return pltpu.CompilerParams(dimension_semantics=sem, vmem_limit_bytes=VMEM_LIMIT)


def _split2(x):
    hi = x.astype(BF16)
    lo = (x - hi.astype(F32)).astype(BF16)
    return hi, lo


def _split3(x):
    hi = x.astype(BF16)
    r = x - hi.astype(F32)
    mid = r.astype(BF16)
    lo = (r - mid.astype(F32)).astype(BF16)
    return hi, mid, lo


def _dot(a, b):
    return jnp.dot(a, b, preferred_element_type=F32)


def _nt(a, b):
    return lax.dot_general(a, b, (((1,), (1,)), ((), ())), preferred_element_type=F32)


def _tn(a, b):
    return lax.dot_general(a, b, (((0,), (0,)), ((), ())), preferred_element_type=F32)


def _rms(x, gain):
    return x * lax.rsqrt(jnp.mean(x * x, axis=-1, keepdims=True) + NORM_EPS) * gain


def _softplus(z):
    return jnp.maximum(z, 0.0) + jnp.log(1.0 + jnp.exp(-jnp.abs(z)))


def _sigmoid(z):
    return 1.0 / (1.0 + jnp.exp(-z))


def _pair_ones():
    r = lax.broadcasted_iota(jnp.int32, (LANES, LANES), 0) // RW_HEAD_DIM
    c = lax.broadcasted_iota(jnp.int32, (LANES, LANES), 1) // RW_HEAD_DIM
    return (r == c).astype(BF16)


def _seg64_sum(x, pair):
    hi, lo = _split2(x)
    return _dot(hi, pair) + _dot(lo, pair)


def _ffn_kernel(h_ref, gpre_ref, wg_ref, wu_ref, wd_ref, gpost_ref, o_ref, u_ref):
    f = pl.program_id(1)

    @pl.when(f == 0)
    def _():
        u_ref[...] = _rms(h_ref[...], gpre_ref[...]).astype(BF16)
        o_ref[...] = jnp.zeros_like(o_ref)

    u = u_ref[...]
    g = _dot(u, wg_ref[...])
    up = _dot(u, wu_ref[...])
    hid = (g * _sigmoid(g) * up).astype(BF16)
    o_ref[...] += _dot(hid, wd_ref[...])

    @pl.when(f == pl.num_programs(1) - 1)
    def _():
        o_ref[...] = h_ref[...] + 0.5 * _rms(o_ref[...], gpost_ref[...])


def _ffn(h, gpre, wg, wu, wd, gpost):
    m, d = h.shape
    ff = wg.shape[1]
    tm = min(FFN_TM, m)
    return pl.pallas_call(
        _ffn_kernel,
        grid=(m // tm, ff // FFN_TF),
        in_specs=[
            pl.BlockSpec((tm, d), lambda i, f: (i, 0)),
            pl.BlockSpec((1, d), lambda i, f: (0, 0)),
            pl.BlockSpec((d, FFN_TF), lambda i, f: (0, f)),
            pl.BlockSpec((d, FFN_TF), lambda i, f: (0, f)),
            pl.BlockSpec((FFN_TF, d), lambda i, f: (f, 0)),
            pl.BlockSpec((1, d), lambda i, f: (0, 0)),
        ],
        out_specs=pl.BlockSpec((tm, d), lambda i, f: (i, 0)),
        out_shape=jax.ShapeDtypeStruct((m, d), F32),
        scratch_shapes=[pltpu.VMEM((tm, d), BF16)],
        compiler_params=_cparams("parallel", "arbitrary"),
        name="ffn",
    )(h, gpre, wg, wu, wd, gpost)


def _norm_kernel(x_ref, g_ref, o_ref):
    o_ref[...] = _rms(x_ref[...], g_ref[...]).astype(o_ref.dtype)


def _norm(x, gain):
    m, d = x.shape
    tm = min(MM_TM, m)
    return pl.pallas_call(
        _norm_kernel,
        grid=(m // tm,),
        in_specs=[pl.BlockSpec((tm, d), lambda i: (i, 0)), pl.BlockSpec((1, d), lambda i: (0, 0))],
        out_specs=pl.BlockSpec((tm, d), lambda i: (i, 0)),
        out_shape=jax.ShapeDtypeStruct((m, d), BF16),
        compiler_params=_cparams("parallel"),
        name="mix_norm",
    )(x, gain)


def _mm_kernel(x_ref, w_ref, o_ref, *, act):
    y = _dot(x_ref[...], w_ref[...])
    if act == "sigmoid":
        y = _sigmoid(y)
    o_ref[...] = y.astype(o_ref.dtype)


def _mm(x, w, out_dtype, act=None, name="mm"):
    m, k = x.shape
    n = w.shape[1]
    tm = min(MM_TM, m)
    tn = min(MM_TN, n)
    return pl.pallas_call(
        functools.partial(_mm_kernel, act=act),
        grid=(m // tm, n // tn),
        in_specs=[pl.BlockSpec((tm, k), lambda i, j: (i, 0)), pl.BlockSpec((k, tn), lambda i, j: (0, j))],
        out_specs=pl.BlockSpec((tm, tn), lambda i, j: (i, j)),
        out_shape=jax.ShapeDtypeStruct((m, n), out_dtype),
        compiler_params=_cparams("parallel", "parallel"),
        name=name,
    )(x, w)


def _sb_kernel(q_ref, k_ref, v_ref, o_ref, acc_ref, car_ref):
    i = pl.program_id(2)
    tq, tk = SB_TQ, SB_TK
    nkb = tq // tk
    head0 = lax.broadcasted_iota(jnp.int32, (tq, LANES), 1) < SB_HEAD_DIM
    q = q_ref[0] * jnp.asarray(SB_HEAD_DIM ** -0.5, BF16)
    zero = jnp.zeros_like(q)
    qh = (jnp.where(head0, q, zero), jnp.where(head0, zero, q))
    row = lax.broadcasted_iota(jnp.int32, (tq, tk), 0)
    col = lax.broadcasted_iota(jnp.int32, (tq, tk), 1)
    r2 = lax.broadcasted_iota(jnp.int32, (tk, 2 * tk), 0)
    c2 = lax.broadcasted_iota(jnp.int32, (tk, 2 * tk), 1)
    tri = ((r2 > c2) | (c2 >= tk)).astype(BF16)

    acc_ref[...] = jnp.zeros_like(acc_ref)
    car_ref[...] = jnp.zeros_like(car_ref)

    def group(g, masked):
        start = pl.multiple_of(g * tq, tq)
        kg = k_ref[0, pl.ds(start, tq), :]
        vg = v_ref[0, pl.ds(start, tq), :]
        order = [(h, u) for u in reversed(range(nkb)) for h in range(2)]
        zg = [_nt(qh[h], kg) for h in range(2)]
        z = {(h, u): zg[h][:, u * tk:(u + 1) * tk] for h, u in order}
        zb = {c: z[c].astype(BF16) for c in order}
        nsp = {c: jnp.minimum(-zb[c], 0.0) - jnp.log(1.0 + jnp.exp(-jnp.abs(zb[c]))) for c in order}
        if masked:
            causal = {u: col + u * tk < row for u in range(nkb)}
            l1m = {c: jnp.where(causal[c[1]], nsp[c], jnp.zeros_like(nsp[c])) for c in order}
        else:
            l1m = nsp
        suf = {c: _dot(l1m[c], tri) for c in order}
        for h in range(2):
            car = car_ref[h]
            w = [None] * nkb
            for u in reversed(range(nkb)):
                wu = jnp.exp((z[h, u] + nsp[h, u].astype(F32)) + suf[h, u][:, :tk] + car)
                if masked:
                    wu = jnp.where(causal[u], wu, 0.0)
                w[u] = wu.astype(BF16)
                car = car + suf[h, u][:, tk:]
            car_ref[h] = car
            acc_ref[h] += _dot(jnp.concatenate(w, axis=1), vg)

    group(i, True)

    def body(j, c):
        group(i - 1 - j, False)
        return c

    lax.fori_loop(0, i, body, 0)
    o_ref[0] = jnp.where(head0, acc_ref[0], acc_ref[1]).astype(o_ref.dtype)


def _sb_attention(p_sb, b, t):
    npair = SB_WIDTH // LANES
    return pl.pallas_call(
        _sb_kernel,
        grid=(b, npair, t // SB_TQ),
        in_specs=[
            pl.BlockSpec((1, SB_TQ, LANES), lambda bi, hp, i: (bi, i, hp)),
            pl.BlockSpec((1, t, LANES), lambda bi, hp, i: (bi, 0, npair + hp)),
            pl.BlockSpec((1, t, LANES), lambda bi, hp, i: (bi, 0, 2 * npair + hp)),
        ],
        out_specs=pl.BlockSpec((1, SB_TQ, LANES), lambda bi, hp, i: (bi, i, hp)),
        out_shape=jax.ShapeDtypeStruct((b, t, SB_WIDTH), BF16),
        scratch_shapes=[pltpu.VMEM((2, SB_TQ, LANES), F32), pltpu.VMEM((2, SB_TQ, SB_TK), F32)],
        compiler_params=_cparams("parallel", "parallel", "arbitrary"),
        name="sb_attention",
    )(p_sb, p_sb, p_sb)


def _hg_kernel(q_ref, f_ref, i_ref, g_ref, lb_ref, on_ref, o_ref, st_ref):
    c = pl.program_id(1)
    n = HG_SUB
    nb = q_ref.shape[0]
    nj = q_ref.shape[2] // LANES
    chains = [(bi, j) for bi in range(nb) for j in range(nj)]

    @pl.when(c == 0)
    def _():
        st_ref[...] = jnp.zeros_like(st_ref)

    lanes = [slice(j * LANES, (j + 1) * LANES) for j in range(nj)]
    lb = [lb_ref[0][:, ls] for ls in lanes]
    gain = [on_ref[0][:, ls] for ls in lanes]
    log_lb = [jnp.log(jnp.maximum(x, LB_FLOOR)) for x in lb]
    log_1m_lb = [jnp.log(1.0 - x) for x in lb]
    one_m_lb = [1.0 - x for x in lb]
    r = lax.broadcasted_iota(jnp.int32, (n, n), 0)
    cidx = lax.broadcasted_iota(jnp.int32, (n, n), 1)
    ltri = (cidx <= r).astype(BF16)
    rown = lax.broadcasted_iota(jnp.int32, (n, 1), 0)

    def sub(step, carry):
        s0 = pl.multiple_of(step * n, n)
        sl = {(bi, j): (bi, pl.ds(s0, n), pl.ds(j * LANES, LANES)) for bi, j in chains}
        qr = {ch: q_ref[sl[ch]] for ch in chains}
        fr = {ch: f_ref[sl[ch]] for ch in chains}
        vi = {ch: i_ref[sl[ch]] for ch in chains}
        q = {ch: qr[ch] * _sigmoid(qr[ch]) for ch in chains}
        xs = {ch: log_1m_lb[ch[1]] - _softplus(-fr[ch]) for ch in chains}
        log_f = {ch: jnp.maximum(log_lb[ch[1]], xs[ch])
                 + jnp.log(1.0 + jnp.exp(-jnp.abs(log_lb[ch[1]] - xs[ch]))) for ch in chains}
        k = {ch: one_m_lb[ch[1]] * _sigmoid(-fr[ch]) for ch in chains}
        parts = {ch: _split3(log_f[ch]) for ch in chains}
        bc = {ch: _dot(ltri, parts[ch][0]) + _dot(ltri, parts[ch][1]) + _dot(ltri, parts[ch][2])
              for ch in chains}
        b_end = {ch: bc[ch][n - 1:n, :] for ch in chains}
        st = {ch: st_ref[ch[0] * nj + ch[1]] for ch in chains}
        o = {ch: _nt((q[ch] * jnp.exp(bc[ch])).astype(BF16), st[ch].astype(BF16)) for ch in chains}
        ke = {ch: (k[ch] * jnp.exp(b_end[ch] - bc[ch])).astype(BF16) for ch in chains}
        for ch in chains:
            st_ref[ch[0] * nj + ch[1]] = st[ch] * jnp.exp(b_end[ch]) + _tn(vi[ch].astype(BF16), ke[ch])
        for s in range(n):
            for ch in chains:
                d = jnp.exp(jnp.minimum(bc[ch] - bc[ch][s:s + 1, :], 0.0))
                col = jnp.sum(q[ch] * d * k[ch][s:s + 1, :], axis=-1, keepdims=True)
                o[ch] = o[ch] + jnp.where(rown >= s, col, 0.0) * vi[ch][s:s + 1, :]
        for ch in chains:
            on = o[ch] * lax.rsqrt(jnp.mean(o[ch] * o[ch], axis=-1, keepdims=True) + NORM_EPS)
            gr = g_ref[sl[ch]]
            o_ref[sl[ch]] = (on * gain[ch[1]] * (gr * _sigmoid(gr))).astype(o_ref.dtype)
        return carry

    lax.fori_loop(0, q_ref.shape[1] // n, sub, 0)


def _hgrn2(p_hg, lb, out_norm, b, t):
    tc = min(HG_TC, t)
    ng = HG_WIDTH // HG_LW

    def spec(off):
        return pl.BlockSpec((b, tc, HG_LW), lambda h, c: (0, c, off * ng + h))

    pspec = pl.BlockSpec((1, 1, HG_LW), lambda h, c: (h, 0, 0))
    return pl.pallas_call(
        _hg_kernel,
        grid=(ng, t // tc),
        in_specs=[spec(0), spec(1), spec(2), spec(3), pspec, pspec],
        out_specs=pl.BlockSpec((b, tc, HG_LW), lambda h, c: (0, c, h)),
        out_shape=jax.ShapeDtypeStruct((b, t, HG_WIDTH), BF16),
        scratch_shapes=[pltpu.VMEM((b * (HG_LW // LANES), HG_DIM, HG_DIM), F32)],
        compiler_params=_cparams("parallel", "arbitrary"),
        name="hgrn2",
    )(p_hg, p_hg, p_hg, p_hg, lb.reshape(ng, 1, HG_LW), out_norm.reshape(ng, 1, HG_LW))


def _rw_prep_kernel(*refs, t_len, mix):
    if mix:
        (x_ref, xp_ref, l_ref, lp_ref, mux_ref, mul_ref, w0_ref, a0_ref, kk_ref, ka_ref,
         w2_ref, a2_ref, g2_ref, vf_ref, v0_ref, v1_ref, v2_ref,
         r_o, w_o, k_o, v_o, an_o, bn_o, g_o) = refs
    else:
        (x_ref, xp_ref, l_ref, lp_ref, mux_ref, mul_ref, w0_ref, a0_ref, kk_ref, ka_ref,
         w2_ref, a2_ref, g2_ref,
         r_o, w_o, k_o, v_o, an_o, bn_o, g_o) = refs
    i = pl.program_id(0)
    tm = x_ref.shape[0]
    keep = ((i * tm) % t_len != 0).astype(F32)
    row0 = lax.broadcasted_iota(jnp.int32, (tm, 1), 0) == 0

    def shift(cur, prev8, mu):
        prev = jnp.where(row0, prev8[7:8, :] * keep, pltpu.roll(cur, 1, 0))
        return cur + (prev - cur) * mu

    x = shift(x_ref[...], xp_ref[...], mux_ref[...])
    xl = shift(l_ref[...], lp_ref[...], mul_ref[...])
    r = x[:, :RW_WIDTH]
    k = x[:, RW_WIDTH:2 * RW_WIDTH]
    v = x[:, 2 * RW_WIDTH:]
    lw = w0_ref[...] + _dot(jnp.tanh(xl).astype(BF16), w2_ref[...])
    log_w = -_softplus(-lw) - 0.5
    w_o[...] = -jnp.exp(log_w)
    a = _sigmoid(a0_ref[...] + _dot(xl.astype(BF16), a2_ref[...]))
    g_o[...] = _dot(_sigmoid(xl).astype(BF16), g2_ref[...])
    if mix:
        vv = _dot(_dot(v.astype(BF16), v1_ref[...]).astype(BF16), v2_ref[...])
        v = v + (vf_ref[...] - v) * _sigmoid(v0_ref[...] + vv)
    pair = _pair_ones()
    kk = k * kk_ref[...]
    sq = kk * kk
    ss = jnp.concatenate([_seg64_sum(sq[:, j * LANES:(j + 1) * LANES], pair)
                          for j in range(RW_WIDTH // LANES)], axis=1)
    kk = kk / jnp.maximum(jnp.sqrt(ss), L2_EPS)
    r_o[...] = r
    k_o[...] = k * (1.0 + (a - 1.0) * ka_ref[...])
    v_o[...] = v
    an_o[...] = -kk
    bn_o[...] = kk * a


def _rw_prep(p_x, p_l, t_len, mu_x, mu_l, w0, a0, k_k, k_a, w2p, a2p, g2p, mv=None):
    m = p_x.shape[0]
    tm = min(PREP_TM, m)
    nb8 = tm // 8
    row = lambda i: (i, 0)
    prev = lambda i: (jnp.maximum(i * nb8 - 1, 0), 0)
    const = lambda i: (0, 0)
    wx, wl = p_x.shape[1], p_l.shape[1]
    vec = pl.BlockSpec((1, RW_WIDTH), const)
    in_specs = [
        pl.BlockSpec((tm, wx), row), pl.BlockSpec((8, wx), prev),
        pl.BlockSpec((tm, wl), row), pl.BlockSpec((8, wl), prev),
        pl.BlockSpec((1, wx), const), pl.BlockSpec((1, wl), const),
        vec, vec, vec, vec,
        pl.BlockSpec((wl, RW_WIDTH), const), pl.BlockSpec((wl, RW_WIDTH), const),
        pl.BlockSpec((wl, RW_WIDTH), const),
    ]
    args = [p_x, p_x, p_l, p_l, mu_x, mu_l, w0, a0, k_k, k_a, w2p, a2p, g2p]
    if mv is not None:
        v_first, v0, v1p, v2p = mv
        in_specs += [pl.BlockSpec((tm, RW_WIDTH), row), vec,
                     pl.BlockSpec((RW_WIDTH, RW_MV_PAD), const), pl.BlockSpec((RW_MV_PAD, RW_WIDTH), const)]
        args += [v_first, v0, v1p, v2p]
    out = jax.ShapeDtypeStruct((m, RW_WIDTH), F32)
    return pl.pallas_call(
        functools.partial(_rw_prep_kernel, t_len=t_len, mix=mv is not None),
        grid=(m // tm,),
        in_specs=in_specs,
        out_specs=[pl.BlockSpec((tm, RW_WIDTH), row)] * 7,
        out_shape=[out] * 7,
        compiler_params=_cparams("parallel"),
        name="rwkv7_prep",
    )(*args)


def _rw_scan_kernel(r_ref, lw_ref, k_ref, v_ref, a_ref, b_ref, g_ref, rk_ref, lnw_ref, lnb_ref,
                    o_ref, st_ref):
    c = pl.program_id(1)
    nb = r_ref.shape[0]
    tc = r_ref.shape[1]
    nj = r_ref.shape[2] // LANES
    cs = RW_CHUNK
    n = RW_HEAD_DIM

    @pl.when(c == 0)
    def _():
        st_ref[...] = jnp.zeros_like(st_ref)

    pair = _pair_ones()
    head0 = lax.broadcasted_iota(jnp.int32, (cs, LANES), 1) < n
    ri = lax.broadcasted_iota(jnp.int32, (cs, cs), 0)
    ci = lax.broadcasted_iota(jnp.int32, (cs, cs), 1)
    strict = ci < ri
    incl = ci <= ri
    ltri = incl.astype(BF16)
    eye = (ri == ci).astype(F32)
    same_head = (lax.broadcasted_iota(jnp.int32, (LANES, LANES), 0) // n
                 == lax.broadcasted_iota(jnp.int32, (LANES, LANES), 1) // n)
    inv_n = 1.0 / n

    def bsum(x):
        hi, lo = _split2(x)
        return _dot(hi, pair) + _dot(lo, pair)

    def chunk(ic, carry):
        t0 = pl.multiple_of(ic * cs, cs)
        chains = range(nb * nj)
        heads = [(i, h) for i in chains for h in range(2)]
        sl = [(i // nj, pl.ds(t0, cs), pl.ds((i % nj) * LANES, LANES)) for i in chains]
        r, lw, k, v, a, b = ([ref[s] for s in sl] for ref in (r_ref, lw_ref, k_ref, v_ref, a_ref, b_ref))
        parts = [_split3(x) for x in lw]
        cl = [_dot(ltri, p[0]) + _dot(ltri, p[1]) + _dot(ltri, p[2]) for p in parts]
        cl_end = [x[cs - 1:cs, :] for x in cl]
        e_neg = [jnp.exp(-x) for x in cl]
        e_tail = [jnp.exp(cl_end[i] - cl[i]) for i in chains]
        abar = [(a[i] * jnp.exp(cl[i] - lw[i])).astype(BF16) for i in chains]
        rbar = [(r[i] * jnp.exp(cl[i])).astype(BF16) for i in chains]
        bbar = [(b[i] * e_neg[i]).astype(BF16) for i in chains]
        kbar = [(k[i] * e_neg[i]).astype(BF16) for i in chains]
        bhat = [(b[i] * e_tail[i]).astype(BF16) for i in chains]
        khat = [(k[i] * e_tail[i]).astype(BF16) for i in chains]
        vb = [x.astype(BF16) for x in v]
        zero = jnp.zeros((cs, LANES), BF16)
        not0 = jnp.logical_not(head0)
        ah = {(i, h): jnp.where(head0 if h == 0 else not0, abar[i], zero) for i, h in heads}
        rh = {(i, h): jnp.where(head0 if h == 0 else not0, rbar[i], zero) for i, h in heads}
        mab = {c: jnp.where(strict, _nt(ah[c], bbar[c[0]]), 0.0) for c in heads}
        mak = {c: jnp.where(strict, _nt(ah[c], kbar[c[0]]), 0.0).astype(BF16) for c in heads}
        nrb = {c: jnp.where(incl, _nt(rh[c], bbar[c[0]]), 0.0).astype(BF16) for c in heads}
        nrk = {c: jnp.where(incl, _nt(rh[c], kbar[c[0]]), 0.0).astype(BF16) for c in heads}
        x = {c: eye + mab[c] for c in heads}
        p = mab
        for _ in range(RW_INV_STEPS):
            pb = {c: p[c].astype(BF16) for c in heads}
            p = {c: _dot(pb[c], pb[c]) for c in heads}
            x = {c: x[c] + _dot(x[c].astype(BF16), p[c].astype(BF16)) for c in heads}
        inv = {c: x[c].astype(BF16) for c in heads}
        st = [st_ref[i] for i in chains]
        sb = [s.astype(BF16) for s in st]
        wb = [(_nt(abar[i], sb[i]) + jnp.where(head0, _dot(mak[i, 0], vb[i]), _dot(mak[i, 1], vb[i]))).astype(BF16)
              for i in chains]
        ub = [jnp.where(head0, _dot(inv[i, 0], wb[i]), _dot(inv[i, 1], wb[i])).astype(BF16) for i in chains]
        y = [_nt(rbar[i], sb[i]) + jnp.where(head0, _dot(nrb[i, 0], ub[i]) + _dot(nrk[i, 0], vb[i]),
                                             _dot(nrb[i, 1], ub[i]) + _dot(nrk[i, 1], vb[i])) for i in chains]
        for i in chains:
            st_ref[i] = st[i] * jnp.exp(cl_end[i]) + jnp.where(same_head, _tn(ub[i], bhat[i]) + _tn(vb[i], khat[i]), 0.0)
        mean = [bsum(x) * inv_n for x in y]
        d = [y[i] - mean[i] for i in chains]
        var = [bsum(x * x) * inv_n for x in d]
        par = [slice((i % nj) * LANES, (i % nj + 1) * LANES) for i in chains]
        bonus = [bsum(r[i] * k[i] * rk_ref[0][:, par[i]]) for i in chains]
        for i in chains:
            yn = (d[i] * lax.rsqrt(var[i] + RW_GN_EPS) * lnw_ref[0][:, par[i]] + lnb_ref[0][:, par[i]]
                  + bonus[i] * v[i])
            o_ref[sl[i]] = (yn * g_ref[sl[i]]).astype(o_ref.dtype)
        return carry

    lax.fori_loop(0, tc // cs, chunk, 0)


def _rw_scan(r, lw, k, v, an, bn, g, r_k, ln_w, ln_b):
    b, t, width = r.shape
    tc = min(RW_TC, t)
    ngrp = width // RW_LW
    seq = pl.BlockSpec((b, tc, RW_LW), lambda hg, c: (0, c, hg))
    par = pl.BlockSpec((1, 1, RW_LW), lambda hg, c: (hg, 0, 0))
    resh = lambda p: p.reshape(ngrp, 1, RW_LW)
    return pl.pallas_call(
        _rw_scan_kernel,
        grid=(ngrp, t // tc),
        in_specs=[seq] * 7 + [par] * 3,
        out_specs=seq,
        out_shape=jax.ShapeDtypeStruct((b, t, width), BF16),
        scratch_shapes=[pltpu.VMEM((b * (RW_LW // LANES), LANES, LANES), F32)],
        compiler_params=_cparams("parallel", "arbitrary"),
        name="rwkv7_scan",
    )(r, lw, k, v, an, bn, g, resh(r_k), resh(ln_w), resh(ln_b))


def _merge_kernel(ysb_ref, yhg_ref, yrw_ref, wsb_ref, whg_ref, wrw_ref, g0_ref, g1_ref, g2_ref, o_ref):
    m = g0_ref[...] * _dot(ysb_ref[...], wsb_ref[...])
    m += g1_ref[...] * _dot(yhg_ref[...], whg_ref[...])
    m += g2_ref[...] * _dot(yrw_ref[...], wrw_ref[...])
    o_ref[...] = m.astype(o_ref.dtype)


def _merge(y_sb, y_hg, y_rw, w_sb, w_hg, w_rw, gates):
    m, kdim = y_sb.shape
    d = w_sb.shape[1]
    tm = min(MM_TM, m)
    tn = MM_TN
    nj = d // tn
    ys = pl.BlockSpec((tm, kdim), lambda i, j: (i, 0))
    ws = pl.BlockSpec((kdim, tn), lambda i, j: (0, j))

    def gs(br):
        return pl.BlockSpec((tm, tn), lambda i, j: (i, br * nj + j))

    return pl.pallas_call(
        _merge_kernel,
        grid=(m // tm, nj),
        in_specs=[ys, ys, ys, ws, ws, ws, gs(0), gs(1), gs(2)],
        out_specs=pl.BlockSpec((tm, tn), lambda i, j: (i, j)),
        out_shape=jax.ShapeDtypeStruct((m, d), BF16),
        compiler_params=_cparams("parallel", "parallel"),
        name="branch_merge",
    )(y_sb, y_hg, y_rw, w_sb, w_hg, w_rw, gates, gates, gates)


def _outproj_kernel(x_ref, w_ref, gain_ref, h_ref, o_ref):
    o_ref[...] = h_ref[...] + _rms(_dot(x_ref[...], w_ref[...]), gain_ref[...])


def _outproj(x, w, gain, h):
    m, d = h.shape
    tm = min(OUT_TM, m)
    row = lambda i: (i, 0)
    const = lambda i: (0, 0)
    return pl.pallas_call(
        _outproj_kernel,
        grid=(m // tm,),
        in_specs=[pl.BlockSpec((tm, d), row), pl.BlockSpec((d, d), const), pl.BlockSpec((1, d), const),
                  pl.BlockSpec((tm, d), row)],
        out_specs=pl.BlockSpec((tm, d), row),
        out_shape=jax.ShapeDtypeStruct((m, d), F32),
        compiler_params=_cparams("parallel"),
        name="out_proj",
    )(x, w, gain, h)


def _cast_kernel(x_ref, o_ref):
    o_ref[...] = x_ref[0].astype(o_ref.dtype)


def _cast_layer(w, l):
    _, r, c = w.shape
    tr = r
    while tr * c * 4 > CAST_BLOCK_BYTES and tr % 32 == 0:
        tr //= 2
    return pl.pallas_call(
        _cast_kernel,
        grid=(r // tr,),
        in_specs=[pl.BlockSpec((1, tr, c), lambda i: (l, i, 0))],
        out_specs=pl.BlockSpec((tr, c), lambda i: (i, 0)),
        out_shape=jax.ShapeDtypeStruct((r, c), BF16),
        compiler_params=_cparams("parallel"),
        name="weight_cast",
    )(w)


def _pad_rows(w, start, total):
    return jnp.zeros((total, w.shape[1]), w.dtype).at[start:start + w.shape[0]].set(w)


def _mixer(h, b, t, l, gain, w_in, lower_bound, hg_out_norm, rw, mv, v_first, w_up_sb, w_up_hg, w_up_rw, w_o,
           post_gain):
    (mu, w0, w2, a0, a2, g2, k_k, k_a, r_k, ln_w, ln_b) = rw
    m = b * t
    row = lambda p: p.reshape(1, -1)
    u = _norm(h, row(gain))
    o_sb = 3 * SB_WIDTH
    o_hg = o_sb + 4 * HG_WIDTH
    o_rx = o_hg + 3 * RW_WIDTH
    o_rl = o_rx + RW_LORA
    wb = _cast_layer(w_in, l)
    p_sb = _mm(u, wb[:, :o_sb], BF16, name="in_proj_sb")
    p_hg = _mm(u, wb[:, o_sb:o_hg], F32, name="in_proj_hg")
    p_rx = _mm(u, wb[:, o_hg:o_rx], F32, name="in_proj_rw")
    w_l = jnp.pad(wb[:, o_rx:o_rl], ((0, 0), (0, RW_LORA_PAD - RW_LORA)))
    p_rl = _mm(u, w_l, F32, name="in_proj_rw_lora")
    gates = _mm(u, wb[:, o_rl:], BF16, act="sigmoid", name="in_proj_gates")

    y_sb = _sb_attention(p_sb.reshape(b, t, o_sb), b, t).reshape(m, SB_WIDTH)
    y_hg = _hgrn2(p_hg.reshape(b, t, 4 * HG_WIDTH), lower_bound, hg_out_norm, b, t).reshape(m, HG_WIDTH)

    mu_x = row(mu[:3 * RW_WIDTH])
    mu_l = row(jnp.pad(mu[3 * RW_WIDTH:], (0, RW_LORA_PAD - RW_LORA)))
    w2p = _pad_rows(w2, 0, RW_LORA_PAD).astype(BF16)
    a2p = _pad_rows(a2, RW_DECAY_LORA, RW_LORA_PAD).astype(BF16)
    g2p = _pad_rows(g2, RW_DECAY_LORA + RW_AAA_LORA, RW_LORA_PAD).astype(BF16)
    if mv is None:
        mvp = None
    else:
        v0, v1, v2 = mv
        mvp = (v_first, row(v0), jnp.pad(v1, ((0, 0), (0, RW_MV_PAD - RW_MV_LORA))).astype(BF16),
               _pad_rows(v2, 0, RW_MV_PAD).astype(BF16))
    r, w, k, v, an, bn, g = _rw_prep(p_rx, p_rl, t, mu_x, mu_l, row(w0), row(a0), row(k_k), row(k_a),
                                     w2p, a2p, g2p, mvp)
    if mv is None:
        v_first = v
    sh = lambda z: z.reshape(b, t, RW_WIDTH)
    y_rw = _rw_scan(sh(r), sh(w), sh(k), sh(v), sh(an), sh(bn), sh(g), r_k.reshape(-1), ln_w, ln_b)
    y_rw = y_rw.reshape(m, RW_WIDTH)

    merged = _merge(y_sb, y_hg, y_rw, _cast_layer(w_up_sb, l), _cast_layer(w_up_hg, l), _cast_layer(w_up_rw, l),
                    gates)
    return _outproj(merged, _cast_layer(w_o, l), row(post_gain), h), v_first


def kernel(x, ffn1_pre_norm, ffn1_w_gate, ffn1_w_up, ffn1_w_down, ffn1_post_norm, mix_pre_norm, w_in, hg_lb_logits, hg_out_norm, rw_mu, rw_w0, rw_w2, rw_a0, rw_a2, rw_g2, rw_v0, rw_v1, rw_v2, rw_k_k, rw_k_a, rw_r_k, rw_ln_w, rw_ln_b, w_up_sb, w_up_hg, w_up_rw, w_o, mix_post_norm, ffn2_pre_norm, ffn2_w_gate, ffn2_w_up, ffn2_w_down, ffn2_post_norm):
    b, t, d = x.shape
    depth = w_in.shape[0]
    lb_w = jax.nn.softmax(hg_lb_logits.astype(F32), axis=0)
    lower_bounds = jnp.cumsum(lb_w, axis=0) - lb_w[0]
    row = lambda p: p.reshape(1, -1)
    h = x.reshape(b * t, d)
    v_first = None
    for l in range(depth):
        h = _ffn(h, row(ffn1_pre_norm[l]), _cast_layer(ffn1_w_gate, l), _cast_layer(ffn1_w_up, l),
                 _cast_layer(ffn1_w_down, l), row(ffn1_post_norm[l]))
        rw = (rw_mu[l], rw_w0[l], rw_w2[l], rw_a0[l], rw_a2[l], rw_g2[l], rw_k_k[l], rw_k_a[l], rw_r_k[l],
              rw_ln_w[l], rw_ln_b[l])
        mv = None if l == 0 else (rw_v0[l - 1], rw_v1[l - 1], rw_v2[l - 1])
        h, v_first = _mixer(h, b, t, l, mix_pre_norm[l], w_in, lower_bounds[l], hg_out_norm[l], rw, mv, v_first,
                            w_up_sb, w_up_hg, w_up_rw, w_o, mix_post_norm[l])
        h = _ffn(h, row(ffn2_pre_norm[l]), _cast_layer(ffn2_w_gate, l), _cast_layer(ffn2_w_up, l),
                 _cast_layer(ffn2_w_down, l), row(ffn2_post_norm[l]))
    return h.reshape(b, t, d)
```

```python
import functools

import jax
import jax.numpy as jnp
from jax import lax
from jax.experimental import pallas as pl
from jax.experimental.pallas import tpu as pltpu

F32 = jnp.float32
BF16 = jnp.bfloat16

D_MODEL = 2048
D_FF = 5632
NORM_EPS = 1e-6
SB_HEADS = 16
SB_HEAD_DIM = 64
SB_WIDTH = SB_HEADS * SB_HEAD_DIM
HG_HEADS = 8
HG_DIM = 128
HG_WIDTH = HG_HEADS * HG_DIM
LB_FLOOR = 1e-20
RW_HEADS = 16
RW_HEAD_DIM = 64
RW_WIDTH = RW_HEADS * RW_HEAD_DIM
RW_DECAY_LORA = 96
RW_AAA_LORA = 96
RW_GATE_LORA = 256
RW_LORA = RW_DECAY_LORA + RW_AAA_LORA + RW_GATE_LORA
RW_LORA_PAD = 512
RW_MV_LORA = 64
RW_MV_PAD = 128
RW_GN_EPS = 64e-5
L2_EPS = 1e-12
N_BRANCH = 3

LANES = 128
SUBLANES = 8
VMEM_LIMIT = 56 * 1024 * 1024

FFN_TM = 512
FFN_TF = 512
MM_TM = 1024
MM_TN = 1024
OUT_TM = 256
SB_TQ = 512
SB_TK = 128
SB_GROUPS = 1
HG_TC = 256
HG_SUB = 16
HG_LW = 256
RW_TC = 256
RW_LW = 256
RW_CHUNK = 64
RW_INV_STEPS = 5
PREP_TM = 256
CAST_BLOCK_BYTES = 6 * 1024 * 1024


def _cparams(*sem):
    return pltpu.CompilerParams(dimension_semantics=sem, vmem_limit_bytes=VMEM_LIMIT)


def _split2(x):
    hi = x.astype(BF16)
    lo = (x - hi.astype(F32)).astype(BF16)
    return hi, lo


def _split3(x):
    hi = x.astype(BF16)
    r = x - hi.astype(F32)
    mid = r.astype(BF16)
    lo = (r - mid.astype(F32)).astype(BF16)
    return hi, mid, lo


def _dot(a, b):
    return jnp.dot(a, b, preferred_element_type=F32)


def _nt(a, b):
    return lax.dot_general(a, b, (((1,), (1,)), ((), ())), preferred_element_type=F32)


def _tn(a, b):
    return lax.dot_general(a, b, (((0,), (0,)), ((), ())), preferred_element_type=F32)


def _rms(x, gain):
    return x * lax.rsqrt(jnp.mean(x * x, axis=-1, keepdims=True) + NORM_EPS) * gain


def _softplus(z):
    return jnp.maximum(z, 0.0) + jnp.log(1.0 + jnp.exp(-jnp.abs(z)))


def _sigmoid(z):
    return 1.0 / (1.0 + jnp.exp(-z))


def _pair_ones():
    r = lax.broadcasted_iota(jnp.int32, (LANES, LANES), 0) // RW_HEAD_DIM
    c = lax.broadcasted_iota(jnp.int32, (LANES, LANES), 1) // RW_HEAD_DIM
    return (r == c).astype(BF16)


def _seg64_sum(x, pair):
    hi, lo = _split2(x)
    return _dot(hi, pair) + _dot(lo, pair)


def _ffn_kernel(h_ref, gpre_ref, wg_ref, wu_ref, wd_ref, gpost_ref, o_ref, u_ref):
    f = pl.program_id(1)

    @pl.when(f == 0)
    def _():
        u_ref[...] = _rms(h_ref[...], gpre_ref[...]).astype(BF16)
        o_ref[...] = jnp.zeros_like(o_ref)

    u = u_ref[...]
    g = _dot(u, wg_ref[...])
    up = _dot(u, wu_ref[...])
    hid = (g * _sigmoid(g) * up).astype(BF16)
    o_ref[...] += _dot(hid, wd_ref[...])

    @pl.when(f == pl.num_programs(1) - 1)
    def _():
        o_ref[...] = h_ref[...] + 0.5 * _rms(o_ref[...], gpost_ref[...])


def _ffn(h, gpre, wg, wu, wd, gpost):
    m, d = h.shape
    ff = wg.shape[1]
    tm = min(FFN_TM, m)
    return pl.pallas_call(
        _ffn_kernel,
        grid=(m // tm, ff // FFN_TF),
        in_specs=[
            pl.BlockSpec((tm, d), lambda i, f: (i, 0)),
            pl.BlockSpec((1, d), lambda i, f: (0, 0)),
            pl.BlockSpec((d, FFN_TF), lambda i, f: (0, f)),
            pl.BlockSpec((d, FFN_TF), lambda i, f: (0, f)),
            pl.BlockSpec((FFN_TF, d), lambda i, f: (f, 0)),
            pl.BlockSpec((1, d), lambda i, f: (0, 0)),
        ],
        out_specs=pl.BlockSpec((tm, d), lambda i, f: (i, 0)),
        out_shape=jax.ShapeDtypeStruct((m, d), F32),
        scratch_shapes=[pltpu.VMEM((tm, d), BF16)],
        compiler_params=_cparams("parallel", "arbitrary"),
        name="ffn",
    )(h, gpre, wg, wu, wd, gpost)


def _norm_kernel(x_ref, g_ref, o_ref):
    o_ref[...] = _rms(x_ref[...], g_ref[...]).astype(o_ref.dtype)


def _norm(x, gain):
    m, d = x.shape
    tm = min(MM_TM, m)
    return pl.pallas_call(
        _norm_kernel,
        grid=(m // tm,),
        in_specs=[pl.BlockSpec((tm, d), lambda i: (i, 0)), pl.BlockSpec((1, d), lambda i: (0, 0))],
        out_specs=pl.BlockSpec((tm, d), lambda i: (i, 0)),
        out_shape=jax.ShapeDtypeStruct((m, d), BF16),
        compiler_params=_cparams("parallel"),
        name="mix_norm",
    )(x, gain)


def _mm_kernel(x_ref, w_ref, o_ref, *, act):
    y = _dot(x_ref[...], w_ref[...])
    if act == "sigmoid":
        y = _sigmoid(y)
    o_ref[...] = y.astype(o_ref.dtype)


def _mm(x, w, out_dtype, act=None, name="mm"):
    m, k = x.shape
    n = w.shape[1]
    tm = min(MM_TM, m)
    tn = min(MM_TN, n)
    return pl.pallas_call(
        functools.partial(_mm_kernel, act=act),
        grid=(m // tm, n // tn),
        in_specs=[pl.BlockSpec((tm, k), lambda i, j: (i, 0)), pl.BlockSpec((k, tn), lambda i, j: (0, j))],
        out_specs=pl.BlockSpec((tm, tn), lambda i, j: (i, j)),
        out_shape=jax.ShapeDtypeStruct((m, n), out_dtype),
        compiler_params=_cparams("parallel", "parallel"),
        name=name,
    )(x, w)


def _sb_kernel(q_ref, k_ref, v_ref, o_ref, acc_ref, car_ref):
    i = pl.program_id(2)
    tq, tk = SB_TQ, SB_TK
    nkb = tq // tk
    head0 = lax.broadcasted_iota(jnp.int32, (tq, LANES), 1) < SB_HEAD_DIM
    q = q_ref[0] * jnp.asarray(SB_HEAD_DIM ** -0.5, BF16)
    zero = jnp.zeros_like(q)
    qh = (jnp.where(head0, q, zero), jnp.where(head0, zero, q))
    row = lax.broadcasted_iota(jnp.int32, (tq, tk), 0)
    col = lax.broadcasted_iota(jnp.int32, (tq, tk), 1)
    r2 = lax.broadcasted_iota(jnp.int32, (tk, 2 * tk), 0)
    c2 = lax.broadcasted_iota(jnp.int32, (tk, 2 * tk), 1)
    tri = ((r2 > c2) | (c2 >= tk)).astype(BF16)

    acc_ref[...] = jnp.zeros_like(acc_ref)
    car_ref[...] = jnp.zeros_like(car_ref)

    def groups(gs, masked):
        ng = range(len(gs))
        starts = [g * tq if isinstance(g, int) else pl.multiple_of(g * tq, tq) for g in gs]
        kg = [k_ref[0, pl.ds(s, tq), :] for s in starts]
        vg = [v_ref[0, pl.ds(s, tq), :] for s in starts]
        order = [(h, gi, u) for gi in ng for u in reversed(range(nkb)) for h in range(2)]
        zg = {(h, gi): _nt(qh[h], kg[gi]) for gi in ng for h in range(2)}
        r0 = [u * tk if masked else 0 for u in range(nkb)]
        z = {(h, gi, u): zg[h, gi][r0[u]:, u * tk:(u + 1) * tk] for h, gi, u in order}
        zb = {c: z[c].astype(BF16) for c in order}
        nsp = {c: jnp.minimum(-zb[c], 0.0) - jnp.log(1.0 + jnp.exp(-jnp.abs(zb[c]))) for c in order}
        if masked:
            causal = [(col < row)[:tq - r0[u]] for u in range(nkb)]
            l1m = {c: jnp.where(causal[c[2]], nsp[c], jnp.zeros_like(nsp[c])) for c in order}
        else:
            l1m = nsp
        suf = {c: _dot(l1m[c], tri) for c in order}
        for h in range(2):
            car = car_ref[h]
            pv = None
            for gi in ng:
                w = [None] * nkb
                for u in reversed(range(nkb)):
                    c = (h, gi, u)
                    wu = jnp.exp((z[c] + nsp[c].astype(F32)) + suf[c][:, :tk] + car[r0[u]:])
                    if masked:
                        wu = jnp.where(causal[u], wu, 0.0)
                    wu = wu.astype(BF16)
                    if r0[u]:
                        wu = jnp.concatenate([jnp.zeros((r0[u], tk), BF16), wu], axis=0)
                        car = jnp.concatenate([car[:r0[u]], car[r0[u]:] + suf[c][:, tk:]], axis=0)
                    else:
                        car = car + suf[c][:, tk:]
                    w[u] = wu
                term = _dot(jnp.concatenate(w, axis=1), vg[gi])
                pv = term if pv is None else pv + term
            car_ref[h] = car
            acc_ref[h] += pv

    groups([i], True)

    def body(j, c):
        g = i - 1 - SB_GROUPS * j
        groups([g - d for d in range(SB_GROUPS)], False)
        return c

    lax.fori_loop(0, i // SB_GROUPS, body, 0)
    for rem in range(1, SB_GROUPS):
        @pl.when(i % SB_GROUPS == rem)
        def _():
            groups([rem - 1 - d for d in range(rem)], False)
    o_ref[0] = jnp.where(head0, acc_ref[0], acc_ref[1]).astype(o_ref.dtype)


def _sb_attention(p_sb, b, t):
    npair = SB_WIDTH // LANES
    return pl.pallas_call(
        _sb_kernel,
        grid=(b, npair, t // SB_TQ),
        in_specs=[
            pl.BlockSpec((1, SB_TQ, LANES), lambda bi, hp, i: (bi, i, hp)),
            pl.BlockSpec((1, t, LANES), lambda bi, hp, i: (bi, 0, npair + hp)),
            pl.BlockSpec((1, t, LANES), lambda bi, hp, i: (bi, 0, 2 * npair + hp)),
        ],
        out_specs=pl.BlockSpec((1, SB_TQ, LANES), lambda bi, hp, i: (bi, i, hp)),
        out_shape=jax.ShapeDtypeStruct((b, t, SB_WIDTH), BF16),
        scratch_shapes=[pltpu.VMEM((2, SB_TQ, LANES), F32), pltpu.VMEM((2, SB_TQ, SB_TK), F32)],
        compiler_params=_cparams("parallel", "parallel", "arbitrary"),
        name="sb_attention",
    )(p_sb, p_sb, p_sb)


def _hg_kernel(q_ref, f_ref, i_ref, g_ref, lb_ref, on_ref, o_ref, st_ref):
    c = pl.program_id(1)
    n = HG_SUB
    nb = q_ref.shape[0]
    nj = q_ref.shape[2] // LANES
    chains = [(bi, j) for bi in range(nb) for j in range(nj)]

    @pl.when(c == 0)
    def _():
        st_ref[...] = jnp.zeros_like(st_ref)

    lanes = [slice(j * LANES, (j + 1) * LANES) for j in range(nj)]
    lb = [lb_ref[0][:, ls] for ls in lanes]
    gain = [on_ref[0][:, ls] for ls in lanes]
    log_lb = [jnp.log(jnp.maximum(x, LB_FLOOR)) for x in lb]
    log_1m_lb = [jnp.log(1.0 - x) for x in lb]
    one_m_lb = [1.0 - x for x in lb]
    r = lax.broadcasted_iota(jnp.int32, (n, n), 0)
    cidx = lax.broadcasted_iota(jnp.int32, (n, n), 1)
    ltri = (cidx <= r).astype(BF16)
    row8 = lax.broadcasted_iota(jnp.int32, (SUBLANES, 1), 0)

    def sub(step, carry):
        s0 = pl.multiple_of(step * n, n)
        sl = {(bi, j): (bi, pl.ds(s0, n), pl.ds(j * LANES, LANES)) for bi, j in chains}
        qr = {ch: q_ref[sl[ch]] for ch in chains}
        fr = {ch: f_ref[sl[ch]] for ch in chains}
        vi = {ch: i_ref[sl[ch]] for ch in chains}
        q = {ch: qr[ch] * _sigmoid(qr[ch]) for ch in chains}
        xs = {ch: log_1m_lb[ch[1]] - _softplus(-fr[ch]) for ch in chains}
        log_f = {ch: jnp.maximum(log_lb[ch[1]], xs[ch])
                 + jnp.log(1.0 + jnp.exp(-jnp.abs(log_lb[ch[1]] - xs[ch]))) for ch in chains}
        k = {ch: one_m_lb[ch[1]] * _sigmoid(-fr[ch]) for ch in chains}
        parts = {ch: _split3(log_f[ch]) for ch in chains}
        bc = {ch: _dot(ltri, parts[ch][0]) + _dot(ltri, parts[ch][1]) + _dot(ltri, parts[ch][2])
              for ch in chains}
        b_end = {ch: bc[ch][n - 1:n, :] for ch in chains}
        st = {ch: st_ref[ch[0] * nj + ch[1]] for ch in chains}
        o = {ch: _nt((q[ch] * jnp.exp(bc[ch])).astype(BF16), st[ch].astype(BF16)) for ch in chains}
        ke = {ch: (k[ch] * jnp.exp(b_end[ch] - bc[ch])).astype(BF16) for ch in chains}
        for ch in chains:
            st_ref[ch[0] * nj + ch[1]] = st[ch] * jnp.exp(b_end[ch]) + _tn(vi[ch].astype(BF16), ke[ch])
        slabs = range(n // SUBLANES)
        rows = lambda x, i: x[i * SUBLANES:(i + 1) * SUBLANES, :]
        os = {ch: [rows(o[ch], i) for i in slabs] for ch in chains}
        for s in range(n):
            for ch in chains:
                for i in slabs:
                    if i < s // SUBLANES:
                        continue
                    d = jnp.exp(jnp.minimum(rows(bc[ch], i) - bc[ch][s:s + 1, :], 0.0))
                    col = jnp.sum(rows(q[ch], i) * d * k[ch][s:s + 1, :], axis=-1, keepdims=True)
                    if i == s // SUBLANES:
                        col = jnp.where(row8 >= s % SUBLANES, col, 0.0)
                    os[ch][i] = os[ch][i] + col * vi[ch][s:s + 1, :]
        o = {ch: jnp.concatenate(os[ch], axis=0) for ch in chains}
        for ch in chains:
            on = o[ch] * lax.rsqrt(jnp.mean(o[ch] * o[ch], axis=-1, keepdims=True) + NORM_EPS)
            gr = g_ref[sl[ch]]
            o_ref[sl[ch]] = (on * gain[ch[1]] * (gr * _sigmoid(gr))).astype(o_ref.dtype)
        return carry

    lax.fori_loop(0, q_ref.shape[1] // n, sub, 0)


def _hgrn2(p_hg, lb, out_norm, b, t):
    tc = min(HG_TC, t)
    ng = HG_WIDTH // HG_LW

    def spec(off):
        return pl.BlockSpec((b, tc, HG_LW), lambda h, c: (0, c, off * ng + h))

    pspec = pl.BlockSpec((1, 1, HG_LW), lambda h, c: (h, 0, 0))
    return pl.pallas_call(
        _hg_kernel,
        grid=(ng, t // tc),
        in_specs=[spec(0), spec(1), spec(2), spec(3), pspec, pspec],
        out_specs=pl.BlockSpec((b, tc, HG_LW), lambda h, c: (0, c, h)),
        out_shape=jax.ShapeDtypeStruct((b, t, HG_WIDTH), BF16),
        scratch_shapes=[pltpu.VMEM((b * (HG_LW // LANES), HG_DIM, HG_DIM), F32)],
        compiler_params=_cparams("parallel", "arbitrary"),
        name="hgrn2",
    )(p_hg, p_hg, p_hg, p_hg, lb.reshape(ng, 1, HG_LW), out_norm.reshape(ng, 1, HG_LW))


def _rw_prep_kernel(*refs, t_len, mix):
    if mix:
        (x_ref, xp_ref, l_ref, lp_ref, mux_ref, mul_ref, w0_ref, a0_ref, kk_ref, ka_ref,
         w2_ref, a2_ref, g2_ref, vf_ref, v0_ref, v1_ref, v2_ref,
         r_o, w_o, k_o, v_o, an_o, bn_o, g_o) = refs
    else:
        (x_ref, xp_ref, l_ref, lp_ref, mux_ref, mul_ref, w0_ref, a0_ref, kk_ref, ka_ref,
         w2_ref, a2_ref, g2_ref,
         r_o, w_o, k_o, v_o, an_o, bn_o, g_o) = refs
    i = pl.program_id(0)
    tm = x_ref.shape[0]
    keep = ((i * tm) % t_len != 0).astype(F32)
    row0 = lax.broadcasted_iota(jnp.int32, (tm, 1), 0) == 0

    def shift(cur, prev8, mu):
        prev = jnp.where(row0, prev8[7:8, :] * keep, pltpu.roll(cur, 1, 0))
        return cur + (prev - cur) * mu

    x = shift(x_ref[...], xp_ref[...], mux_ref[...])
    xl = shift(l_ref[...], lp_ref[...], mul_ref[...])
    r = x[:, :RW_WIDTH]
    k = x[:, RW_WIDTH:2 * RW_WIDTH]
    v = x[:, 2 * RW_WIDTH:]
    lw = w0_ref[...] + _dot(jnp.tanh(xl).astype(BF16), w2_ref[...])
    log_w = -_softplus(-lw) - 0.5
    w_o[...] = -jnp.exp(log_w)
    a = _sigmoid(a0_ref[...] + _dot(xl.astype(BF16), a2_ref[...]))
    g_o[...] = _dot(_sigmoid(xl).astype(BF16), g2_ref[...])
    if mix:
        vv = _dot(_dot(v.astype(BF16), v1_ref[...]).astype(BF16), v2_ref[...])
        v = v + (vf_ref[...] - v) * _sigmoid(v0_ref[...] + vv)
    pair = _pair_ones()
    kk = k * kk_ref[...]
    sq = kk * kk
    ss = jnp.concatenate([_seg64_sum(sq[:, j * LANES:(j + 1) * LANES], pair)
                          for j in range(RW_WIDTH // LANES)], axis=1)
    kk = kk / jnp.maximum(jnp.sqrt(ss), L2_EPS)
    r_o[...] = r
    k_o[...] = k * (1.0 + (a - 1.0) * ka_ref[...])
    v_o[...] = v
    an_o[...] = -kk
    bn_o[...] = kk * a


def _rw_prep(p_x, p_l, t_len, mu_x, mu_l, w0, a0, k_k, k_a, w2p, a2p, g2p, mv=None):
    m = p_x.shape[0]
    tm = min(PREP_TM, m)
    nb8 = tm // 8
    row = lambda i: (i, 0)
    prev = lambda i: (jnp.maximum(i * nb8 - 1, 0), 0)
    const = lambda i: (0, 0)
    wx, wl = p_x.shape[1], p_l.shape[1]
    vec = pl.BlockSpec((1, RW_WIDTH), const)
    in_specs = [
        pl.BlockSpec((tm, wx), row), pl.BlockSpec((8, wx), prev),
        pl.BlockSpec((tm, wl), row), pl.BlockSpec((8, wl), prev),
        pl.BlockSpec((1, wx), const), pl.BlockSpec((1, wl), const),
        vec, vec, vec, vec,
        pl.BlockSpec((wl, RW_WIDTH), const), pl.BlockSpec((wl, RW_WIDTH), const),
        pl.BlockSpec((wl, RW_WIDTH), const),
    ]
    args = [p_x, p_x, p_l, p_l, mu_x, mu_l, w0, a0, k_k, k_a, w2p, a2p, g2p]
    if mv is not None:
        v_first, v0, v1p, v2p = mv
        in_specs += [pl.BlockSpec((tm, RW_WIDTH), row), vec,
                     pl.BlockSpec((RW_WIDTH, RW_MV_PAD), const), pl.BlockSpec((RW_MV_PAD, RW_WIDTH), const)]
        args += [v_first, v0, v1p, v2p]
    out = jax.ShapeDtypeStruct((m, RW_WIDTH), F32)
    return pl.pallas_call(
        functools.partial(_rw_prep_kernel, t_len=t_len, mix=mv is not None),
        grid=(m // tm,),
        in_specs=in_specs,
        out_specs=[pl.BlockSpec((tm, RW_WIDTH), row)] * 7,
        out_shape=[out] * 7,
        compiler_params=_cparams("parallel"),
        name="rwkv7_prep",
    )(*args)


def _rw_scan_kernel(r_ref, lw_ref, k_ref, v_ref, a_ref, b_ref, g_ref, rk_ref, lnw_ref, lnb_ref,
                    o_ref, st_ref):
    c = pl.program_id(1)
    nb = r_ref.shape[0]
    tc = r_ref.shape[1]
    nj = r_ref.shape[2] // LANES
    cs = RW_CHUNK
    n = RW_HEAD_DIM

    @pl.when(c == 0)
    def _():
        st_ref[...] = jnp.zeros_like(st_ref)

    pair = _pair_ones()
    head0 = lax.broadcasted_iota(jnp.int32, (cs, LANES), 1) < n
    ri = lax.broadcasted_iota(jnp.int32, (cs, cs), 0)
    ci = lax.broadcasted_iota(jnp.int32, (cs, cs), 1)
    strict = ci < ri
    incl = ci <= ri
    ltri = incl.astype(BF16)
    eye = (ri == ci).astype(F32)
    same_head = (lax.broadcasted_iota(jnp.int32, (LANES, LANES), 0) // n
                 == lax.broadcasted_iota(jnp.int32, (LANES, LANES), 1) // n)
    inv_n = 1.0 / n

    def bsum(x):
        hi, lo = _split2(x)
        return _dot(hi, pair) + _dot(lo, pair)

    def chunk(ic, carry):
        t0 = pl.multiple_of(ic * cs, cs)
        chains = range(nb * nj)
        heads = [(i, h) for i in chains for h in range(2)]
        sl = [(i // nj, pl.ds(t0, cs), pl.ds((i % nj) * LANES, LANES)) for i in chains]
        r, lw, k, v, a, b = ([ref[s] for s in sl] for ref in (r_ref, lw_ref, k_ref, v_ref, a_ref, b_ref))
        parts = [_split3(x) for x in lw]
        cl = [_dot(ltri, p[0]) + _dot(ltri, p[1]) + _dot(ltri, p[2]) for p in parts]
        cl_end = [x[cs - 1:cs, :] for x in cl]
        e_neg = [jnp.exp(-x) for x in cl]
        e_tail = [jnp.exp(cl_end[i] - cl[i]) for i in chains]
        abar = [(a[i] * jnp.exp(cl[i] - lw[i])).astype(BF16) for i in chains]
        rbar = [(r[i] * jnp.exp(cl[i])).astype(BF16) for i in chains]
        bbar = [(b[i] * e_neg[i]).astype(BF16) for i in chains]
        kbar = [(k[i] * e_neg[i]).astype(BF16) for i in chains]
        bhat = [(b[i] * e_tail[i]).astype(BF16) for i in chains]
        khat = [(k[i] * e_tail[i]).astype(BF16) for i in chains]
        vb = [x.astype(BF16) for x in v]
        zero = jnp.zeros((cs, LANES), BF16)
        not0 = jnp.logical_not(head0)
        ah = {(i, h): jnp.where(head0 if h == 0 else not0, abar[i], zero) for i, h in heads}
        rh = {(i, h): jnp.where(head0 if h == 0 else not0, rbar[i], zero) for i, h in heads}
        mab = {c: jnp.where(strict, _nt(ah[c], bbar[c[0]]), 0.0) for c in heads}
        mak = {c: jnp.where(strict, _nt(ah[c], kbar[c[0]]), 0.0).astype(BF16) for c in heads}
        nrb = {c: jnp.where(incl, _nt(rh[c], bbar[c[0]]), 0.0).astype(BF16) for c in heads}
        nrk = {c: jnp.where(incl, _nt(rh[c], kbar[c[0]]), 0.0).astype(BF16) for c in heads}
        x = {c: eye + mab[c] for c in heads}
        p = mab
        for _ in range(RW_INV_STEPS):
            pb = {c: p[c].astype(BF16) for c in heads}
            p = {c: _dot(pb[c], pb[c]) for c in heads}
            x = {c: x[c] + _dot(x[c].astype(BF16), p[c].astype(BF16)) for c in heads}
        inv = {c: x[c].astype(BF16) for c in heads}
        st = [st_ref[i] for i in chains]
        sb = [s.astype(BF16) for s in st]
        wb = [(_nt(abar[i], sb[i]) + jnp.where(head0, _dot(mak[i, 0], vb[i]), _dot(mak[i, 1], vb[i]))).astype(BF16)
              for i in chains]
        ub = [jnp.where(head0, _dot(inv[i, 0], wb[i]), _dot(inv[i, 1], wb[i])).astype(BF16) for i in chains]
        y = [_nt(rbar[i], sb[i]) + jnp.where(head0, _dot(nrb[i, 0], ub[i]) + _dot(nrk[i, 0], vb[i]),
                                             _dot(nrb[i, 1], ub[i]) + _dot(nrk[i, 1], vb[i])) for i in chains]
        for i in chains:
            st_ref[i] = st[i] * jnp.exp(cl_end[i]) + jnp.where(same_head, _tn(ub[i], bhat[i]) + _tn(vb[i], khat[i]), 0.0)
        mean = [bsum(x) * inv_n for x in y]
        d = [y[i] - mean[i] for i in chains]
        var = [bsum(x * x) * inv_n for x in d]
        par = [slice((i % nj) * LANES, (i % nj + 1) * LANES) for i in chains]
        bonus = [bsum(r[i] * k[i] * rk_ref[0][:, par[i]]) for i in chains]
        for i in chains:
            yn = (d[i] * lax.rsqrt(var[i] + RW_GN_EPS) * lnw_ref[0][:, par[i]] + lnb_ref[0][:, par[i]]
                  + bonus[i] * v[i])
            o_ref[sl[i]] = (yn * g_ref[sl[i]]).astype(o_ref.dtype)
        return carry

    lax.fori_loop(0, tc // cs, chunk, 0)


def _rw_scan(r, lw, k, v, an, bn, g, r_k, ln_w, ln_b):
    b, t, width = r.shape
    tc = min(RW_TC, t)
    ngrp = width // RW_LW
    seq = pl.BlockSpec((b, tc, RW_LW), lambda hg, c: (0, c, hg))
    par = pl.BlockSpec((1, 1, RW_LW), lambda hg, c: (hg, 0, 0))
    resh = lambda p: p.reshape(ngrp, 1, RW_LW)
    return pl.pallas_call(
        _rw_scan_kernel,
        grid=(ngrp, t // tc),
        in_specs=[seq] * 7 + [par] * 3,
        out_specs=seq,
        out_shape=jax.ShapeDtypeStruct((b, t, width), BF16),
        scratch_shapes=[pltpu.VMEM((b * (RW_LW // LANES), LANES, LANES), F32)],
        compiler_params=_cparams("parallel", "arbitrary"),
        name="rwkv7_scan",
    )(r, lw, k, v, an, bn, g, resh(r_k), resh(ln_w), resh(ln_b))


def _merge_kernel(ysb_ref, yhg_ref, yrw_ref, wsb_ref, whg_ref, wrw_ref, g0_ref, g1_ref, g2_ref, o_ref):
    m = g0_ref[...] * _dot(ysb_ref[...], wsb_ref[...])
    m += g1_ref[...] * _dot(yhg_ref[...], whg_ref[...])
    m += g2_ref[...] * _dot(yrw_ref[...], wrw_ref[...])
    o_ref[...] = m.astype(o_ref.dtype)


def _merge(y_sb, y_hg, y_rw, w_sb, w_hg, w_rw, gates):
    m, kdim = y_sb.shape
    d = w_sb.shape[1]
    tm = min(MM_TM, m)
    tn = MM_TN
    nj = d // tn
    ys = pl.BlockSpec((tm, kdim), lambda i, j: (i, 0))
    ws = pl.BlockSpec((kdim, tn), lambda i, j: (0, j))

    def gs(br):
        return pl.BlockSpec((tm, tn), lambda i, j: (i, br * nj + j))

    return pl.pallas_call(
        _merge_kernel,
        grid=(m // tm, nj),
        in_specs=[ys, ys, ys, ws, ws, ws, gs(0), gs(1), gs(2)],
        out_specs=pl.BlockSpec((tm, tn), lambda i, j: (i, j)),
        out_shape=jax.ShapeDtypeStruct((m, d), BF16),
        compiler_params=_cparams("parallel", "parallel"),
        name="branch_merge",
    )(y_sb, y_hg, y_rw, w_sb, w_hg, w_rw, gates, gates, gates)


def _outproj_kernel(x_ref, w_ref, gain_ref, h_ref, o_ref):
    o_ref[...] = h_ref[...] + _rms(_dot(x_ref[...], w_ref[...]), gain_ref[...])


def _outproj(x, w, gain, h):
    m, d = h.shape
    tm = min(OUT_TM, m)
    row = lambda i: (i, 0)
    const = lambda i: (0, 0)
    return pl.pallas_call(
        _outproj_kernel,
        grid=(m // tm,),
        in_specs=[pl.BlockSpec((tm, d), row), pl.BlockSpec((d, d), const), pl.BlockSpec((1, d), const),
                  pl.BlockSpec((tm, d), row)],
        out_specs=pl.BlockSpec((tm, d), row),
        out_shape=jax.ShapeDtypeStruct((m, d), F32),
        compiler_params=_cparams("parallel"),
        name="out_proj",
    )(x, w, gain, h)


def _cast_kernel(x_ref, o_ref):
    o_ref[...] = x_ref[0].astype(o_ref.dtype)


def _cast_layer(w, l):
    _, r, c = w.shape
    tr = r
    while tr * c * 4 > CAST_BLOCK_BYTES and tr % 32 == 0:
        tr //= 2
    return pl.pallas_call(
        _cast_kernel,
        grid=(r // tr,),
        in_specs=[pl.BlockSpec((1, tr, c), lambda i: (l, i, 0))],
        out_specs=pl.BlockSpec((tr, c), lambda i: (i, 0)),
        out_shape=jax.ShapeDtypeStruct((r, c), BF16),
        compiler_params=_cparams("parallel"),
        name="weight_cast",
    )(w)


def _pad_rows(w, start, total):
    return jnp.zeros((total, w.shape[1]), w.dtype).at[start:start + w.shape[0]].set(w)


def _mixer(h, b, t, l, gain, w_in, lower_bound, hg_out_norm, rw, mv, v_first, w_up_sb, w_up_hg, w_up_rw, w_o,
           post_gain):
    (mu, w0, w2, a0, a2, g2, k_k, k_a, r_k, ln_w, ln_b) = rw
    m = b * t
    row = lambda p: p.reshape(1, -1)
    u = _norm(h, row(gain))
    o_sb = 3 * SB_WIDTH
    o_hg = o_sb + 4 * HG_WIDTH
    o_rx = o_hg + 3 * RW_WIDTH
    o_rl = o_rx + RW_LORA
    wb = _cast_layer(w_in, l)
    p_sb = _mm(u, wb[:, :o_sb], BF16, name="in_proj_sb")
    p_hg = _mm(u, wb[:, o_sb:o_hg], F32, name="in_proj_hg")
    p_rx = _mm(u, wb[:, o_hg:o_rx], F32, name="in_proj_rw")
    w_l = jnp.pad(wb[:, o_rx:o_rl], ((0, 0), (0, RW_LORA_PAD - RW_LORA)))
    p_rl = _mm(u, w_l, F32, name="in_proj_rw_lora")
    gates = _mm(u, wb[:, o_rl:], BF16, act="sigmoid", name="in_proj_gates")

    y_sb = _sb_attention(p_sb.reshape(b, t, o_sb), b, t).reshape(m, SB_WIDTH)
    y_hg = _hgrn2(p_hg.reshape(b, t, 4 * HG_WIDTH), lower_bound, hg_out_norm, b, t).reshape(m, HG_WIDTH)

    mu_x = row(mu[:3 * RW_WIDTH])
    mu_l = row(jnp.pad(mu[3 * RW_WIDTH:], (0, RW_LORA_PAD - RW_LORA)))
    w2p = _pad_rows(w2, 0, RW_LORA_PAD).astype(BF16)
    a2p = _pad_rows(a2, RW_DECAY_LORA, RW_LORA_PAD).astype(BF16)
    g2p = _pad_rows(g2, RW_DECAY_LORA + RW_AAA_LORA, RW_LORA_PAD).astype(BF16)
    if mv is None:
        mvp = None
    else:
        v0, v1, v2 = mv
        mvp = (v_first, row(v0), jnp.pad(v1, ((0, 0), (0, RW_MV_PAD - RW_MV_LORA))).astype(BF16),
               _pad_rows(v2, 0, RW_MV_PAD).astype(BF16))
    r, w, k, v, an, bn, g = _rw_prep(p_rx, p_rl, t, mu_x, mu_l, row(w0), row(a0), row(k_k), row(k_a),
                                     w2p, a2p, g2p, mvp)
    if mv is None:
        v_first = v
    sh = lambda z: z.reshape(b, t, RW_WIDTH)
    y_rw = _rw_scan(sh(r), sh(w), sh(k), sh(v), sh(an), sh(bn), sh(g), r_k.reshape(-1), ln_w, ln_b)
    y_rw = y_rw.reshape(m, RW_WIDTH)

    merged = _merge(y_sb, y_hg, y_rw, _cast_layer(w_up_sb, l), _cast_layer(w_up_hg, l), _cast_layer(w_up_rw, l),
                    gates)
    return _outproj(merged, _cast_layer(w_o, l), row(post_gain), h), v_first


def kernel(x, ffn1_pre_norm, ffn1_w_gate, ffn1_w_up, ffn1_w_down, ffn1_post_norm, mix_pre_norm, w_in, hg_lb_logits, hg_out_norm, rw_mu, rw_w0, rw_w2, rw_a0, rw_a2, rw_g2, rw_v0, rw_v1, rw_v2, rw_k_k, rw_k_a, rw_r_k, rw_ln_w, rw_ln_b, w_up_sb, w_up_hg, w_up_rw, w_o, mix_post_norm, ffn2_pre_norm, ffn2_w_gate, ffn2_w_up, ffn2_w_down, ffn2_post_norm):
    b, t, d = x.shape
    depth = w_in.shape[0]
    lb_w = jax.nn.softmax(hg_lb_logits.astype(F32), axis=0)
    lower_bounds = jnp.cumsum(lb_w, axis=0) - lb_w[0]
    row = lambda p: p.reshape(1, -1)
    h = x.reshape(b * t, d)
    v_first = None
    for l in range(depth):
        h = _ffn(h, row(ffn1_pre_norm[l]), _cast_layer(ffn1_w_gate, l), _cast_layer(ffn1_w_up, l),
                 _cast_layer(ffn1_w_down, l), row(ffn1_post_norm[l]))
        rw = (rw_mu[l], rw_w0[l], rw_w2[l], rw_a0[l], rw_a2[l], rw_g2[l], rw_k_k[l], rw_k_a[l], rw_r_k[l],
              rw_ln_w[l], rw_ln_b[l])
        mv = None if l == 0 else (rw_v0[l - 1], rw_v1[l - 1], rw_v2[l - 1])
        h, v_first = _mixer(h, b, t, l, mix_pre_norm[l], w_in, lower_bounds[l], hg_out_norm[l], rw, mv, v_first,
                            w_up_sb, w_up_hg, w_up_rw, w_o, mix_post_norm[l])
        h = _ffn(h, row(ffn2_pre_norm[l]), _cast_layer(ffn2_w_gate, l), _cast_layer(ffn2_w_up, l),
                 _cast_layer(ffn2_w_down, l), row(ffn2_post_norm[l]))
    return h.reshape(b, t, d)
```

```python
import functools

import jax
import jax.numpy as jnp
from jax import lax
from jax.experimental import pallas as pl
from jax.experimental.pallas import tpu as pltpu

F32 = jnp.float32
BF16 = jnp.bfloat16

D_MODEL = 2048
D_FF = 5632
NORM_EPS = 1e-6
SB_HEADS = 16
SB_HEAD_DIM = 64
SB_WIDTH = SB_HEADS * SB_HEAD_DIM
HG_HEADS = 8
HG_DIM = 128
HG_WIDTH = HG_HEADS * HG_DIM
LB_FLOOR = 1e-20
RW_HEADS = 16
RW_HEAD_DIM = 64
RW_WIDTH = RW_HEADS * RW_HEAD_DIM
RW_DECAY_LORA = 96
RW_AAA_LORA = 96
RW_GATE_LORA = 256
RW_LORA = RW_DECAY_LORA + RW_AAA_LORA + RW_GATE_LORA
RW_LORA_PAD = 512
RW_MV_LORA = 64
RW_MV_PAD = 128
RW_GN_EPS = 64e-5
L2_EPS = 1e-12
N_BRANCH = 3

LANES = 128
SUBLANES = 8
VMEM_LIMIT = 56 * 1024 * 1024

FFN_TM = 512
FFN_TF = 512
MM_TM = 1024
MM_TN = 1024
OUT_TM = 256
SB_TQ = 512
SB_TK = 128
SB_GROUPS = 1
HG_TC = 256
HG_SUB = 16
HG_LW = 256
RW_TC = 256
RW_LW = 256
RW_CHUNK = 64
RW_INV_STEPS = 5
PREP_TM = 256
CAST_BLOCK_BYTES = 6 * 1024 * 1024


def _cparams(*sem):
    return pltpu.CompilerParams(dimension_semantics=sem, vmem_limit_bytes=VMEM_LIMIT)


def _split2(x):
    hi = x.astype(BF16)
    lo = (x - hi.astype(F32)).astype(BF16)
    return hi, lo


def _split3(x):
    hi = x.astype(BF16)
    r = x - hi.astype(F32)
    mid = r.astype(BF16)
    lo = (r - mid.astype(F32)).astype(BF16)
    return hi, mid, lo


def _dot(a, b):
    return jnp.dot(a, b, preferred_element_type=F32)


def _nt(a, b):
    return lax.dot_general(a, b, (((1,), (1,)), ((), ())), preferred_element_type=F32)


def _tn(a, b):
    return lax.dot_general(a, b, (((0,), (0,)), ((), ())), preferred_element_type=F32)


def _rms(x, gain):
    return x * lax.rsqrt(jnp.mean(x * x, axis=-1, keepdims=True) + NORM_EPS) * gain


def _softplus(z):
    return jnp.maximum(z, 0.0) + jnp.log(1.0 + jnp.exp(-jnp.abs(z)))


def _sigmoid(z):
    return 1.0 / (1.0 + jnp.exp(-z))


def _pair_ones():
    r = lax.broadcasted_iota(jnp.int32, (LANES, LANES), 0) // RW_HEAD_DIM
    c = lax.broadcasted_iota(jnp.int32, (LANES, LANES), 1) // RW_HEAD_DIM
    return (r == c).astype(BF16)


def _seg64_sum(x, pair):
    hi, lo = _split2(x)
    return _dot(hi, pair) + _dot(lo, pair)


def _ffn_kernel(h_ref, gpre_ref, wg_ref, wu_ref, wd_ref, gpost_ref, o_ref, u_ref):
    f = pl.program_id(1)

    @pl.when(f == 0)
    def _():
        u_ref[...] = _rms(h_ref[...], gpre_ref[...]).astype(BF16)
        o_ref[...] = jnp.zeros_like(o_ref)

    u = u_ref[...]
    g = _dot(u, wg_ref[...])
    up = _dot(u, wu_ref[...])
    hid = (g * _sigmoid(g) * up).astype(BF16)
    o_ref[...] += _dot(hid, wd_ref[...])

    @pl.when(f == pl.num_programs(1) - 1)
    def _():
        o_ref[...] = h_ref[...] + 0.5 * _rms(o_ref[...], gpost_ref[...])


def _ffn(h, gpre, wg, wu, wd, gpost):
    m, d = h.shape
    ff = wg.shape[1]
    tm = min(FFN_TM, m)
    return pl.pallas_call(
        _ffn_kernel,
        grid=(m // tm, ff // FFN_TF),
        in_specs=[
            pl.BlockSpec((tm, d), lambda i, f: (i, 0)),
            pl.BlockSpec((1, d), lambda i, f: (0, 0)),
            pl.BlockSpec((d, FFN_TF), lambda i, f: (0, f)),
            pl.BlockSpec((d, FFN_TF), lambda i, f: (0, f)),
            pl.BlockSpec((FFN_TF, d), lambda i, f: (f, 0)),
            pl.BlockSpec((1, d), lambda i, f: (0, 0)),
        ],
        out_specs=pl.BlockSpec((tm, d), lambda i, f: (i, 0)),
        out_shape=jax.ShapeDtypeStruct((m, d), F32),
        scratch_shapes=[pltpu.VMEM((tm, d), BF16)],
        compiler_params=_cparams("parallel", "arbitrary"),
        name="ffn",
    )(h, gpre, wg, wu, wd, gpost)


def _norm_kernel(x_ref, g_ref, o_ref):
    o_ref[...] = _rms(x_ref[...], g_ref[...]).astype(o_ref.dtype)


def _norm(x, gain):
    m, d = x.shape
    tm = min(MM_TM, m)
    return pl.pallas_call(
        _norm_kernel,
        grid=(m // tm,),
        in_specs=[pl.BlockSpec((tm, d), lambda i: (i, 0)), pl.BlockSpec((1, d), lambda i: (0, 0))],
        out_specs=pl.BlockSpec((tm, d), lambda i: (i, 0)),
        out_shape=jax.ShapeDtypeStruct((m, d), BF16),
        compiler_params=_cparams("parallel"),
        name="mix_norm",
    )(x, gain)


def _mm_kernel(x_ref, w_ref, o_ref, *, act):
    y = _dot(x_ref[...], w_ref[...])
    if act == "sigmoid":
        y = _sigmoid(y)
    o_ref[...] = y.astype(o_ref.dtype)


def _mm(x, w, out_dtype, act=None, name="mm"):
    m, k = x.shape
    n = w.shape[1]
    tm = min(MM_TM, m)
    tn = min(MM_TN, n)
    return pl.pallas_call(
        functools.partial(_mm_kernel, act=act),
        grid=(m // tm, n // tn),
        in_specs=[pl.BlockSpec((tm, k), lambda i, j: (i, 0)), pl.BlockSpec((k, tn), lambda i, j: (0, j))],
        out_specs=pl.BlockSpec((tm, tn), lambda i, j: (i, j)),
        out_shape=jax.ShapeDtypeStruct((m, n), out_dtype),
        compiler_params=_cparams("parallel", "parallel"),
        name=name,
    )(x, w)


def _sb_kernel(q_ref, k_ref, v_ref, o_ref, acc_ref, car_ref):
    i = pl.program_id(2)
    tq, tk = SB_TQ, SB_TK
    nkb = tq // tk
    head0 = lax.broadcasted_iota(jnp.int32, (tq, LANES), 1) < SB_HEAD_DIM
    q = q_ref[0] * jnp.asarray(SB_HEAD_DIM ** -0.5, BF16)
    zero = jnp.zeros_like(q)
    qh = (jnp.where(head0, q, zero), jnp.where(head0, zero, q))
    row = lax.broadcasted_iota(jnp.int32, (tq, tk), 0)
    col = lax.broadcasted_iota(jnp.int32, (tq, tk), 1)
    r2 = lax.broadcasted_iota(jnp.int32, (tk, 2 * tk), 0)
    c2 = lax.broadcasted_iota(jnp.int32, (tk, 2 * tk), 1)
    tri = ((r2 > c2) | (c2 >= tk)).astype(BF16)

    acc_ref[...] = jnp.zeros_like(acc_ref)
    car_ref[...] = jnp.zeros_like(car_ref)

    def groups(gs, masked):
        ng = range(len(gs))
        starts = [g * tq if isinstance(g, int) else pl.multiple_of(g * tq, tq) for g in gs]
        kg = [k_ref[0, pl.ds(s, tq), :] for s in starts]
        vg = [v_ref[0, pl.ds(s, tq), :] for s in starts]
        order = [(h, gi, u) for gi in ng for u in reversed(range(nkb)) for h in range(2)]
        zg = {(h, gi): _nt(qh[h], kg[gi]) for gi in ng for h in range(2)}
        r0 = [u * tk if masked else 0 for u in range(nkb)]
        z = {(h, gi, u): zg[h, gi][r0[u]:, u * tk:(u + 1) * tk] for h, gi, u in order}
        zb = {c: z[c].astype(BF16) for c in order}
        tb = {c: jnp.log(1.0 + jnp.exp(-jnp.abs(zb[c]))) for c in order}
        nsp = {c: jnp.minimum(-zb[c], 0.0) - tb[c] for c in order}
        if masked:
            causal = [(col < row)[:tq - r0[u]] for u in range(nkb)]
            l1m = {c: jnp.where(causal[c[2]], nsp[c], jnp.zeros_like(nsp[c])) for c in order}
        else:
            l1m = nsp
        suf = {c: _dot(l1m[c], tri) for c in order}
        for h in range(2):
            car = car_ref[h]
            pv = None
            for gi in ng:
                w = [None] * nkb
                for u in reversed(range(nkb)):
                    c = (h, gi, u)
                    wu = jnp.exp((jnp.minimum(z[c], 0.0) - tb[c].astype(F32)) + suf[c][:, :tk] + car[r0[u]:])
                    if masked:
                        wu = jnp.where(causal[u], wu, 0.0)
                    wu = wu.astype(BF16)
                    if r0[u]:
                        wu = jnp.concatenate([jnp.zeros((r0[u], tk), BF16), wu], axis=0)
                        car = jnp.concatenate([car[:r0[u]], car[r0[u]:] + suf[c][:, tk:]], axis=0)
                    else:
                        car = car + suf[c][:, tk:]
                    w[u] = wu
                term = _dot(jnp.concatenate(w, axis=1), vg[gi])
                pv = term if pv is None else pv + term
            car_ref[h] = car
            acc_ref[h] += pv

    groups([i], True)

    def body(j, c):
        g = i - 1 - SB_GROUPS * j
        groups([g - d for d in range(SB_GROUPS)], False)
        return c

    lax.fori_loop(0, i // SB_GROUPS, body, 0)
    for rem in range(1, SB_GROUPS):
        @pl.when(i % SB_GROUPS == rem)
        def _():
            groups([rem - 1 - d for d in range(rem)], False)
    o_ref[0] = jnp.where(head0, acc_ref[0], acc_ref[1]).astype(o_ref.dtype)


def _sb_attention(p_sb, b, t):
    npair = SB_WIDTH // LANES
    return pl.pallas_call(
        _sb_kernel,
        grid=(b, npair, t // SB_TQ),
        in_specs=[
            pl.BlockSpec((1, SB_TQ, LANES), lambda bi, hp, i: (bi, i, hp)),
            pl.BlockSpec((1, t, LANES), lambda bi, hp, i: (bi, 0, npair + hp)),
            pl.BlockSpec((1, t, LANES), lambda bi, hp, i: (bi, 0, 2 * npair + hp)),
        ],
        out_specs=pl.BlockSpec((1, SB_TQ, LANES), lambda bi, hp, i: (bi, i, hp)),
        out_shape=jax.ShapeDtypeStruct((b, t, SB_WIDTH), BF16),
        scratch_shapes=[pltpu.VMEM((2, SB_TQ, LANES), F32), pltpu.VMEM((2, SB_TQ, SB_TK), F32)],
        compiler_params=_cparams("parallel", "parallel", "arbitrary"),
        name="sb_attention",
    )(p_sb, p_sb, p_sb)


def _hg_kernel(q_ref, f_ref, i_ref, g_ref, lb_ref, on_ref, o_ref, st_ref):
    c = pl.program_id(1)
    n = HG_SUB
    nb = q_ref.shape[0]
    nj = q_ref.shape[2] // LANES
    chains = [(bi, j) for bi in range(nb) for j in range(nj)]

    @pl.when(c == 0)
    def _():
        st_ref[...] = jnp.zeros_like(st_ref)

    lanes = [slice(j * LANES, (j + 1) * LANES) for j in range(nj)]
    lb = [lb_ref[0][:, ls] for ls in lanes]
    gain = [on_ref[0][:, ls] for ls in lanes]
    log_lb = [jnp.log(jnp.maximum(x, LB_FLOOR)) for x in lb]
    log_1m_lb = [jnp.log(1.0 - x) for x in lb]
    one_m_lb = [1.0 - x for x in lb]
    r = lax.broadcasted_iota(jnp.int32, (n, n), 0)
    cidx = lax.broadcasted_iota(jnp.int32, (n, n), 1)
    ltri = (cidx <= r).astype(BF16)
    row8 = lax.broadcasted_iota(jnp.int32, (SUBLANES, 1), 0)

    def sub(step, carry):
        s0 = pl.multiple_of(step * n, n)
        sl = {(bi, j): (bi, pl.ds(s0, n), pl.ds(j * LANES, LANES)) for bi, j in chains}
        qr = {ch: q_ref[sl[ch]] for ch in chains}
        fr = {ch: f_ref[sl[ch]] for ch in chains}
        vi = {ch: i_ref[sl[ch]] for ch in chains}
        q = {ch: qr[ch] * _sigmoid(qr[ch]) for ch in chains}
        xs = {ch: log_1m_lb[ch[1]] - _softplus(-fr[ch]) for ch in chains}
        log_f = {ch: jnp.maximum(log_lb[ch[1]], xs[ch])
                 + jnp.log(1.0 + jnp.exp(-jnp.abs(log_lb[ch[1]] - xs[ch]))) for ch in chains}
        k = {ch: one_m_lb[ch[1]] * _sigmoid(-fr[ch]) for ch in chains}
        parts = {ch: _split3(log_f[ch]) for ch in chains}
        bc = {ch: _dot(ltri, parts[ch][0]) + _dot(ltri, parts[ch][1]) + _dot(ltri, parts[ch][2])
              for ch in chains}
        b_end = {ch: bc[ch][n - 1:n, :] for ch in chains}
        st = {ch: st_ref[ch[0] * nj + ch[1]] for ch in chains}
        o = {ch: _nt((q[ch] * jnp.exp(bc[ch])).astype(BF16), st[ch].astype(BF16)) for ch in chains}
        ke = {ch: (k[ch] * jnp.exp(b_end[ch] - bc[ch])).astype(BF16) for ch in chains}
        for ch in chains:
            st_ref[ch[0] * nj + ch[1]] = st[ch] * jnp.exp(b_end[ch]) + _tn(vi[ch].astype(BF16), ke[ch])
        slabs = range(n // SUBLANES)
        rows = lambda x, i: x[i * SUBLANES:(i + 1) * SUBLANES, :]
        os = {ch: [rows(o[ch], i) for i in slabs] for ch in chains}
        for s in range(n):
            for ch in chains:
                for i in slabs:
                    if i < s // SUBLANES:
                        continue
                    d = jnp.exp(jnp.minimum(rows(bc[ch], i) - bc[ch][s:s + 1, :], 0.0))
                    col = jnp.sum(rows(q[ch], i) * d * k[ch][s:s + 1, :], axis=-1, keepdims=True)
                    if i == s // SUBLANES:
                        col = jnp.where(row8 >= s % SUBLANES, col, 0.0)
                    os[ch][i] = os[ch][i] + col * vi[ch][s:s + 1, :]
        o = {ch: jnp.concatenate(os[ch], axis=0) for ch in chains}
        for ch in chains:
            on = o[ch] * lax.rsqrt(jnp.mean(o[ch] * o[ch], axis=-1, keepdims=True) + NORM_EPS)
            gr = g_ref[sl[ch]]
            o_ref[sl[ch]] = (on * gain[ch[1]] * (gr * _sigmoid(gr))).astype(o_ref.dtype)
        return carry

    lax.fori_loop(0, q_ref.shape[1] // n, sub, 0)


def _hgrn2(p_hg, lb, out_norm, b, t):
    tc = min(HG_TC, t)
    ng = HG_WIDTH // HG_LW

    def spec(off):
        return pl.BlockSpec((b, tc, HG_LW), lambda h, c: (0, c, off * ng + h))

    pspec = pl.BlockSpec((1, 1, HG_LW), lambda h, c: (h, 0, 0))
    return pl.pallas_call(
        _hg_kernel,
        grid=(ng, t // tc),
        in_specs=[spec(0), spec(1), spec(2), spec(3), pspec, pspec],
        out_specs=pl.BlockSpec((b, tc, HG_LW), lambda h, c: (0, c, h)),
        out_shape=jax.ShapeDtypeStruct((b, t, HG_WIDTH), BF16),
        scratch_shapes=[pltpu.VMEM((b * (HG_LW // LANES), HG_DIM, HG_DIM), F32)],
        compiler_params=_cparams("parallel", "arbitrary"),
        name="hgrn2",
    )(p_hg, p_hg, p_hg, p_hg, lb.reshape(ng, 1, HG_LW), out_norm.reshape(ng, 1, HG_LW))


def _rw_prep_kernel(*refs, t_len, mix):
    if mix:
        (x_ref, xp_ref, l_ref, lp_ref, mux_ref, mul_ref, w0_ref, a0_ref, kk_ref, ka_ref,
         w2_ref, a2_ref, g2_ref, vf_ref, v0_ref, v1_ref, v2_ref,
         r_o, w_o, k_o, v_o, an_o, bn_o, g_o) = refs
    else:
        (x_ref, xp_ref, l_ref, lp_ref, mux_ref, mul_ref, w0_ref, a0_ref, kk_ref, ka_ref,
         w2_ref, a2_ref, g2_ref,
         r_o, w_o, k_o, v_o, an_o, bn_o, g_o) = refs
    i = pl.program_id(0)
    tm = x_ref.shape[0]
    keep = ((i * tm) % t_len != 0).astype(F32)
    row0 = lax.broadcasted_iota(jnp.int32, (tm, 1), 0) == 0

    def shift(cur, prev8, mu):
        prev = jnp.where(row0, prev8[7:8, :] * keep, pltpu.roll(cur, 1, 0))
        return cur + (prev - cur) * mu

    x = shift(x_ref[...], xp_ref[...], mux_ref[...])
    xl = shift(l_ref[...], lp_ref[...], mul_ref[...])
    r = x[:, :RW_WIDTH]
    k = x[:, RW_WIDTH:2 * RW_WIDTH]
    v = x[:, 2 * RW_WIDTH:]
    lw = w0_ref[...] + _dot(jnp.tanh(xl).astype(BF16), w2_ref[...])
    log_w = -_softplus(-lw) - 0.5
    w_o[...] = -jnp.exp(log_w)
    a = _sigmoid(a0_ref[...] + _dot(xl.astype(BF16), a2_ref[...]))
    g_o[...] = _dot(_sigmoid(xl).astype(BF16), g2_ref[...])
    if mix:
        vv = _dot(_dot(v.astype(BF16), v1_ref[...]).astype(BF16), v2_ref[...])
        v = v + (vf_ref[...] - v) * _sigmoid(v0_ref[...] + vv)
    pair = _pair_ones()
    kk = k * kk_ref[...]
    sq = kk * kk
    ss = jnp.concatenate([_seg64_sum(sq[:, j * LANES:(j + 1) * LANES], pair)
                          for j in range(RW_WIDTH // LANES)], axis=1)
    kk = kk / jnp.maximum(jnp.sqrt(ss), L2_EPS)
    r_o[...] = r
    k_o[...] = k * (1.0 + (a - 1.0) * ka_ref[...])
    v_o[...] = v
    an_o[...] = -kk
    bn_o[...] = kk * a


def _rw_prep(p_x, p_l, t_len, mu_x, mu_l, w0, a0, k_k, k_a, w2p, a2p, g2p, mv=None):
    m = p_x.shape[0]
    tm = min(PREP_TM, m)
    nb8 = tm // 8
    row = lambda i: (i, 0)
    prev = lambda i: (jnp.maximum(i * nb8 - 1, 0), 0)
    const = lambda i: (0, 0)
    wx, wl = p_x.shape[1], p_l.shape[1]
    vec = pl.BlockSpec((1, RW_WIDTH), const)
    in_specs = [
        pl.BlockSpec((tm, wx), row), pl.BlockSpec((8, wx), prev),
        pl.BlockSpec((tm, wl), row), pl.BlockSpec((8, wl), prev),
        pl.BlockSpec((1, wx), const), pl.BlockSpec((1, wl), const),
        vec, vec, vec, vec,
        pl.BlockSpec((wl, RW_WIDTH), const), pl.BlockSpec((wl, RW_WIDTH), const),
        pl.BlockSpec((wl, RW_WIDTH), const),
    ]
    args = [p_x, p_x, p_l, p_l, mu_x, mu_l, w0, a0, k_k, k_a, w2p, a2p, g2p]
    if mv is not None:
        v_first, v0, v1p, v2p = mv
        in_specs += [pl.BlockSpec((tm, RW_WIDTH), row), vec,
                     pl.BlockSpec((RW_WIDTH, RW_MV_PAD), const), pl.BlockSpec((RW_MV_PAD, RW_WIDTH), const)]
        args += [v_first, v0, v1p, v2p]
    out = jax.ShapeDtypeStruct((m, RW_WIDTH), F32)
    return pl.pallas_call(
        functools.partial(_rw_prep_kernel, t_len=t_len, mix=mv is not None),
        grid=(m // tm,),
        in_specs=in_specs,
        out_specs=[pl.BlockSpec((tm, RW_WIDTH), row)] * 7,
        out_shape=[out] * 7,
        compiler_params=_cparams("parallel"),
        name="rwkv7_prep",
    )(*args)


def _rw_scan_kernel(r_ref, lw_ref, k_ref, v_ref, a_ref, b_ref, g_ref, rk_ref, lnw_ref, lnb_ref,
                    o_ref, st_ref):
    c = pl.program_id(1)
    nb = r_ref.shape[0]
    tc = r_ref.shape[1]
    nj = r_ref.shape[2] // LANES
    cs = RW_CHUNK
    n = RW_HEAD_DIM

    @pl.when(c == 0)
    def _():
        st_ref[...] = jnp.zeros_like(st_ref)

    pair = _pair_ones()
    head0 = lax.broadcasted_iota(jnp.int32, (cs, LANES), 1) < n
    ri = lax.broadcasted_iota(jnp.int32, (cs, cs), 0)
    ci = lax.broadcasted_iota(jnp.int32, (cs, cs), 1)
    strict = ci < ri
    incl = ci <= ri
    ltri = incl.astype(BF16)
    eye = (ri == ci).astype(F32)
    same_head = (lax.broadcasted_iota(jnp.int32, (LANES, LANES), 0) // n
                 == lax.broadcasted_iota(jnp.int32, (LANES, LANES), 1) // n)
    inv_n = 1.0 / n

    def bsum(x):
        hi, lo = _split2(x)
        return _dot(hi, pair) + _dot(lo, pair)

    def chunk(ic, carry):
        t0 = pl.multiple_of(ic * cs, cs)
        chains = range(nb * nj)
        heads = [(i, h) for i in chains for h in range(2)]
        sl = [(i // nj, pl.ds(t0, cs), pl.ds((i % nj) * LANES, LANES)) for i in chains]
        r, lw, k, v, a, b = ([ref[s] for s in sl] for ref in (r_ref, lw_ref, k_ref, v_ref, a_ref, b_ref))
        parts = [_split3(x) for x in lw]
        cl = [_dot(ltri, p[0]) + _dot(ltri, p[1]) + _dot(ltri, p[2]) for p in parts]
        cl_end = [x[cs - 1:cs, :] for x in cl]
        e_neg = [jnp.exp(-x) for x in cl]
        e_tail = [jnp.exp(cl_end[i] - cl[i]) for i in chains]
        abar = [(a[i] * jnp.exp(cl[i] - lw[i])).astype(BF16) for i in chains]
        rbar = [(r[i] * jnp.exp(cl[i])).astype(BF16) for i in chains]
        bbar = [(b[i] * e_neg[i]).astype(BF16) for i in chains]
        kbar = [(k[i] * e_neg[i]).astype(BF16) for i in chains]
        bhat = [(b[i] * e_tail[i]).astype(BF16) for i in chains]
        khat = [(k[i] * e_tail[i]).astype(BF16) for i in chains]
        vb = [x.astype(BF16) for x in v]
        zero = jnp.zeros((cs, LANES), BF16)
        not0 = jnp.logical_not(head0)
        ah = {(i, h): jnp.where(head0 if h == 0 else not0, abar[i], zero) for i, h in heads}
        rh = {(i, h): jnp.where(head0 if h == 0 else not0, rbar[i], zero) for i, h in heads}
        mab = {c: jnp.where(strict, _nt(ah[c], bbar[c[0]]), 0.0) for c in heads}
        mak = {c: jnp.where(strict, _nt(ah[c], kbar[c[0]]), 0.0).astype(BF16) for c in heads}
        nrb = {c: jnp.where(incl, _nt(rh[c], bbar[c[0]]), 0.0).astype(BF16) for c in heads}
        nrk = {c: jnp.where(incl, _nt(rh[c], kbar[c[0]]), 0.0).astype(BF16) for c in heads}
        x = {c: eye + mab[c] for c in heads}
        p = mab
        for _ in range(RW_INV_STEPS):
            pb = {c: p[c].astype(BF16) for c in heads}
            p = {c: _dot(pb[c], pb[c]) for c in heads}
            x = {c: x[c] + _dot(x[c].astype(BF16), p[c].astype(BF16)) for c in heads}
        inv = {c: x[c].astype(BF16) for c in heads}
        st = [st_ref[i] for i in chains]
        sb = [s.astype(BF16) for s in st]
        wb = [(_nt(abar[i], sb[i]) + jnp.where(head0, _dot(mak[i, 0], vb[i]), _dot(mak[i, 1], vb[i]))).astype(BF16)
              for i in chains]
        ub = [jnp.where(head0, _dot(inv[i, 0], wb[i]), _dot(inv[i, 1], wb[i])).astype(BF16) for i in chains]
        y = [_nt(rbar[i], sb[i]) + jnp.where(head0, _dot(nrb[i, 0], ub[i]) + _dot(nrk[i, 0], vb[i]),
                                             _dot(nrb[i, 1], ub[i]) + _dot(nrk[i, 1], vb[i])) for i in chains]
        for i in chains:
            st_ref[i] = st[i] * jnp.exp(cl_end[i]) + jnp.where(same_head, _tn(ub[i], bhat[i]) + _tn(vb[i], khat[i]), 0.0)
        mean = [bsum(x) * inv_n for x in y]
        d = [y[i] - mean[i] for i in chains]
        var = [bsum(x * x) * inv_n for x in d]
        par = [slice((i % nj) * LANES, (i % nj + 1) * LANES) for i in chains]
        bonus = [bsum(r[i] * k[i] * rk_ref[0][:, par[i]]) for i in chains]
        for i in chains:
            yn = (d[i] * lax.rsqrt(var[i] + RW_GN_EPS) * lnw_ref[0][:, par[i]] + lnb_ref[0][:, par[i]]
                  + bonus[i] * v[i])
            o_ref[sl[i]] = (yn * g_ref[sl[i]]).astype(o_ref.dtype)
        return carry

    lax.fori_loop(0, tc // cs, chunk, 0)


def _rw_scan(r, lw, k, v, an, bn, g, r_k, ln_w, ln_b):
    b, t, width = r.shape
    tc = min(RW_TC, t)
    ngrp = width // RW_LW
    seq = pl.BlockSpec((b, tc, RW_LW), lambda hg, c: (0, c, hg))
    par = pl.BlockSpec((1, 1, RW_LW), lambda hg, c: (hg, 0, 0))
    resh = lambda p: p.reshape(ngrp, 1, RW_LW)
    return pl.pallas_call(
        _rw_scan_kernel,
        grid=(ngrp, t // tc),
        in_specs=[seq] * 7 + [par] * 3,
        out_specs=seq,
        out_shape=jax.ShapeDtypeStruct((b, t, width), BF16),
        scratch_shapes=[pltpu.VMEM((b * (RW_LW // LANES), LANES, LANES), F32)],
        compiler_params=_cparams("parallel", "arbitrary"),
        name="rwkv7_scan",
    )(r, lw, k, v, an, bn, g, resh(r_k), resh(ln_w), resh(ln_b))


def _merge_kernel(ysb_ref, yhg_ref, yrw_ref, wsb_ref, whg_ref, wrw_ref, g0_ref, g1_ref, g2_ref, o_ref):
    m = g0_ref[...] * _dot(ysb_ref[...], wsb_ref[...])
    m += g1_ref[...] * _dot(yhg_ref[...], whg_ref[...])
    m += g2_ref[...] * _dot(yrw_ref[...], wrw_ref[...])
    o_ref[...] = m.astype(o_ref.dtype)


def _merge(y_sb, y_hg, y_rw, w_sb, w_hg, w_rw, gates):
    m, kdim = y_sb.shape
    d = w_sb.shape[1]
    tm = min(MM_TM, m)
    tn = MM_TN
    nj = d // tn
    ys = pl.BlockSpec((tm, kdim), lambda i, j: (i, 0))
    ws = pl.BlockSpec((kdim, tn), lambda i, j: (0, j))

    def gs(br):
        return pl.BlockSpec((tm, tn), lambda i, j: (i, br * nj + j))

    return pl.pallas_call(
        _merge_kernel,
        grid=(m // tm, nj),
        in_specs=[ys, ys, ys, ws, ws, ws, gs(0), gs(1), gs(2)],
        out_specs=pl.BlockSpec((tm, tn), lambda i, j: (i, j)),
        out_shape=jax.ShapeDtypeStruct((m, d), BF16),
        compiler_params=_cparams("parallel", "parallel"),
        name="branch_merge",
    )(y_sb, y_hg, y_rw, w_sb, w_hg, w_rw, gates, gates, gates)


def _outproj_kernel(x_ref, w_ref, gain_ref, h_ref, o_ref):
    o_ref[...] = h_ref[...] + _rms(_dot(x_ref[...], w_ref[...]), gain_ref[...])


def _outproj(x, w, gain, h):
    m, d = h.shape
    tm = min(OUT_TM, m)
    row = lambda i: (i, 0)
    const = lambda i: (0, 0)
    return pl.pallas_call(
        _outproj_kernel,
        grid=(m // tm,),
        in_specs=[pl.BlockSpec((tm, d), row), pl.BlockSpec((d, d), const), pl.BlockSpec((1, d), const),
                  pl.BlockSpec((tm, d), row)],
        out_specs=pl.BlockSpec((tm, d), row),
        out_shape=jax.ShapeDtypeStruct((m, d), F32),
        compiler_params=_cparams("parallel"),
        name="out_proj",
    )(x, w, gain, h)


def _cast_kernel(x_ref, o_ref):
    o_ref[...] = x_ref[0].astype(o_ref.dtype)


def _cast_layer(w, l):
    _, r, c = w.shape
    tr = r
    while tr * c * 4 > CAST_BLOCK_BYTES and tr % 32 == 0:
        tr //= 2
    return pl.pallas_call(
        _cast_kernel,
        grid=(r // tr,),
        in_specs=[pl.BlockSpec((1, tr, c), lambda i: (l, i, 0))],
        out_specs=pl.BlockSpec((tr, c), lambda i: (i, 0)),
        out_shape=jax.ShapeDtypeStruct((r, c), BF16),
        compiler_params=_cparams("parallel"),
        name="weight_cast",
    )(w)


def _pad_rows(w, start, total):
    return jnp.zeros((total, w.shape[1]), w.dtype).at[start:start + w.shape[0]].set(w)


def _mixer(h, b, t, l, gain, w_in, lower_bound, hg_out_norm, rw, mv, v_first, w_up_sb, w_up_hg, w_up_rw, w_o,
           post_gain):
    (mu, w0, w2, a0, a2, g2, k_k, k_a, r_k, ln_w, ln_b) = rw
    m = b * t
    row = lambda p: p.reshape(1, -1)
    u = _norm(h, row(gain))
    o_sb = 3 * SB_WIDTH
    o_hg = o_sb + 4 * HG_WIDTH
    o_rx = o_hg + 3 * RW_WIDTH
    o_rl = o_rx + RW_LORA
    wb = _cast_layer(w_in, l)
    p_sb = _mm(u, wb[:, :o_sb], BF16, name="in_proj_sb")
    p_hg = _mm(u, wb[:, o_sb:o_hg], F32, name="in_proj_hg")
    p_rx = _mm(u, wb[:, o_hg:o_rx], F32, name="in_proj_rw")
    w_l = jnp.pad(wb[:, o_rx:o_rl], ((0, 0), (0, RW_LORA_PAD - RW_LORA)))
    p_rl = _mm(u, w_l, F32, name="in_proj_rw_lora")
    gates = _mm(u, wb[:, o_rl:], BF16, act="sigmoid", name="in_proj_gates")

    y_sb = _sb_attention(p_sb.reshape(b, t, o_sb), b, t).reshape(m, SB_WIDTH)
    y_hg = _hgrn2(p_hg.reshape(b, t, 4 * HG_WIDTH), lower_bound, hg_out_norm, b, t).reshape(m, HG_WIDTH)

    mu_x = row(mu[:3 * RW_WIDTH])
    mu_l = row(jnp.pad(mu[3 * RW_WIDTH:], (0, RW_LORA_PAD - RW_LORA)))
    w2p = _pad_rows(w2, 0, RW_LORA_PAD).astype(BF16)
    a2p = _pad_rows(a2, RW_DECAY_LORA, RW_LORA_PAD).astype(BF16)
    g2p = _pad_rows(g2, RW_DECAY_LORA + RW_AAA_LORA, RW_LORA_PAD).astype(BF16)
    if mv is None:
        mvp = None
    else:
        v0, v1, v2 = mv
        mvp = (v_first, row(v0), jnp.pad(v1, ((0, 0), (0, RW_MV_PAD - RW_MV_LORA))).astype(BF16),
               _pad_rows(v2, 0, RW_MV_PAD).astype(BF16))
    r, w, k, v, an, bn, g = _rw_prep(p_rx, p_rl, t, mu_x, mu_l, row(w0), row(a0), row(k_k), row(k_a),
                                     w2p, a2p, g2p, mvp)
    if mv is None:
        v_first = v
    sh = lambda z: z.reshape(b, t, RW_WIDTH)
    y_rw = _rw_scan(sh(r), sh(w), sh(k), sh(v), sh(an), sh(bn), sh(g), r_k.reshape(-1), ln_w, ln_b)
    y_rw = y_rw.reshape(m, RW_WIDTH)

    merged = _merge(y_sb, y_hg, y_rw, _cast_layer(w_up_sb, l), _cast_layer(w_up_hg, l), _cast_layer(w_up_rw, l),
                    gates)
    return _outproj(merged, _cast_layer(w_o, l), row(post_gain), h), v_first


def kernel(x, ffn1_pre_norm, ffn1_w_gate, ffn1_w_up, ffn1_w_down, ffn1_post_norm, mix_pre_norm, w_in, hg_lb_logits, hg_out_norm, rw_mu, rw_w0, rw_w2, rw_a0, rw_a2, rw_g2, rw_v0, rw_v1, rw_v2, rw_k_k, rw_k_a, rw_r_k, rw_ln_w, rw_ln_b, w_up_sb, w_up_hg, w_up_rw, w_o, mix_post_norm, ffn2_pre_norm, ffn2_w_gate, ffn2_w_up, ffn2_w_down, ffn2_post_norm):
    b, t, d = x.shape
    depth = w_in.shape[0]
    lb_w = jax.nn.softmax(hg_lb_logits.astype(F32), axis=0)
    lower_bounds = jnp.cumsum(lb_w, axis=0) - lb_w[0]
    row = lambda p: p.reshape(1, -1)
    h = x.reshape(b * t, d)
    v_first = None
    for l in range(depth):
        h = _ffn(h, row(ffn1_pre_norm[l]), _cast_layer(ffn1_w_gate, l), _cast_layer(ffn1_w_up, l),
                 _cast_layer(ffn1_w_down, l), row(ffn1_post_norm[l]))
        rw = (rw_mu[l], rw_w0[l], rw_w2[l], rw_a0[l], rw_a2[l], rw_g2[l], rw_k_k[l], rw_k_a[l], rw_r_k[l],
              rw_ln_w[l], rw_ln_b[l])
        mv = None if l == 0 else (rw_v0[l - 1], rw_v1[l - 1], rw_v2[l - 1])
        h, v_first = _mixer(h, b, t, l, mix_pre_norm[l], w_in, lower_bounds[l], hg_out_norm[l], rw, mv, v_first,
                            w_up_sb, w_up_hg, w_up_rw, w_o, mix_post_norm[l])
        h = _ffn(h, row(ffn2_pre_norm[l]), _cast_layer(ffn2_w_gate, l), _cast_layer(ffn2_w_up, l),
                 _cast_layer(ffn2_w_down, l), row(ffn2_post_norm[l]))
    return h.reshape(b, t, d)
```
